```python
import math
import jax, jax.numpy as jnp
from jax import lax
import numpy as np

D_MODEL = 1024
BATCH = 2
SEQ = 8192
DEPTH = 4
DEC_BATCH = 128
DEC_SEQ = 4
PAST_LEN = 2048
PAGE_SIZE = 128

MIX = D_MODEL
A_WIDTH = MIX // 2
A_HEADS = 4
A_DH = A_WIDTH // (2 * A_HEADS)
A_QK = A_HEADS * 2 * A_DH
A_V = A_HEADS * 2 * A_DH
ROT_DIM = A_DH // 4
ROPE_THETA = 500000.0
Q_BLOCK = 128
NEG_INF = -1e30
B_WIDTH = MIX // 4
S5_GROUP = 16
S5_GROUPS = B_WIDTH // S5_GROUP
S5_P = 64
DT_MIN = 1e-3
DT_MAX = 1e-1
C_WIDTH = MIX - A_WIDTH - B_WIDTH
GLA_HEADS = 4
GLA_DK = C_WIDTH // 2 // GLA_HEADS
GLA_DV = C_WIDTH // GLA_HEADS
GLA_QK = GLA_HEADS * GLA_DK
GLA_RANK = 16
GLA_TAU = 16.0
GLA_CHUNK = 64
IN_DIM = 2 * A_QK + A_V + B_WIDTH + 2 * GLA_QK + 2 * C_WIDTH + GLA_RANK
D_FF = ((8 * D_MODEL // 3 + 127) // 128) * 128
MACARON_W = 0.5
RMS_EPS = 1e-6
N_SUB = 3

kernel_name = 'hymba_diffattn_s5_gla_macaron_decode_step'


def _rmsnorm(x, g):
    xf = x.astype(jnp.float32)
    y = xf * lax.rsqrt(jnp.mean(xf * xf, axis=-1, keepdims=True) + RMS_EPS)
    return (y * g.astype(jnp.float32)).astype(x.dtype)


def _swiglu(h, wi, wo):
    gate, up = jnp.split(h @ wi, 2, axis=-1)
    return (jax.nn.silu(gate) * up) @ wo


def _split_points():
    sizes = (A_QK, A_QK, A_V, B_WIDTH, GLA_QK, GLA_QK, C_WIDTH, C_WIDTH, GLA_RANK)
    return [int(s) for s in np.cumsum(sizes)[:-1]]


def _partial_rope(x, pos):
    half = ROT_DIM // 2
    inv = ROPE_THETA ** (-jnp.arange(half, dtype=jnp.float32) / half)
    ang = pos.astype(jnp.float32)[:, None] * inv[None, :]
    cos = jnp.cos(ang)[None, :, None, None, :]
    sin = jnp.sin(ang)[None, :, None, None, :]
    xr = x[..., :ROT_DIM].astype(jnp.float32)
    x1, x2 = xr[..., :half], xr[..., half:]
    rot = jnp.concatenate([x1 * cos - x2 * sin, x2 * cos + x1 * sin], axis=-1)
    return jnp.concatenate([rot.astype(x.dtype), x[..., ROT_DIM:]], axis=-1)


def _diff_attention(q, k, v, q_pos, k_pos, lam):
    bn, tq = q.shape[:2]
    L = min(Q_BLOCK, tq)
    nb = -(-tq // L)
    pad = nb * L - tq
    if pad:
        q = jnp.pad(q, ((0, 0), (0, pad), (0, 0), (0, 0), (0, 0)))
        q_pos = jnp.pad(q_pos, (0, pad), mode='edge')
    qs = jnp.moveaxis(q.reshape(bn, nb, L, A_HEADS, 2, A_DH), 1, 0)
    qps = q_pos.reshape(nb, L)
    scale = A_DH ** -0.5

    def one_block(args):
        qb, qp = args
        s = jnp.einsum('bqhcd,bkhcd->bhcqk', qb, k, preferred_element_type=jnp.float32) * scale
        s = jnp.where(k_pos[None, :] <= qp[:, None], s, NEG_INF)
        pr = jax.nn.softmax(s, axis=-1)
        w = pr[:, :, 0] - lam * pr[:, :, 1]
        return jnp.einsum('bhqk,bkhe->bqhe', w, v)

    out = lax.map(one_block, (qs, qps))
    out = jnp.moveaxis(out, 0, 1).reshape(bn, nb * L, A_HEADS, 2 * A_DH)
    return out[:, :tq]


def _lin_combine(e1, e2):
    a1, b1 = e1
    a2, b2 = e2
    return a2 * a1, a2 * b1 + b2


def _s5(u, x0_re, x0_im, lam_re, lam_im, b_re, b_im, c_re, c_im, d, log_dt, glu_w, glu_b):
    f32 = jnp.float32
    bn, t, _ = u.shape
    ug = u.astype(f32).reshape(bn, t, S5_GROUPS, S5_GROUP)
    lam = lax.complex(lam_re.astype(f32), lam_im.astype(f32))
    dt = jnp.exp(log_dt.astype(f32))[:, None]
    lam_bar = jnp.exp(lam * dt)
    b_bar = ((lam_bar - 1.0) / lam)[..., None] * lax.complex(b_re.astype(f32), b_im.astype(f32))
    bu = jnp.einsum('gpc,btgc->btgp', b_bar, ug.astype(jnp.complex64))
    if x0_re is not None:
        x0 = lax.complex(x0_re.astype(f32), x0_im.astype(f32))
        bu = bu.at[:, 0].add(lam_bar * x0)
    a = jnp.broadcast_to(lam_bar, bu.shape)
    _, xs = lax.associative_scan(_lin_combine, (a, bu), axis=1)
    cm = lax.complex(c_re.astype(f32), c_im.astype(f32))
    y = jnp.einsum('gcp,btgp->btgc', cm, xs).real + d.astype(f32).reshape(S5_GROUPS, S5_GROUP) * ug
    y = jax.nn.gelu(y.reshape(bn, t, B_WIDTH))
    y = y * jax.nn.sigmoid(y @ glu_w.astype(f32) + glu_b.astype(f32))
    x_last = xs[:, -1]
    return y, x_last.real, x_last.imag


def _gla(q, k, v, log_a, s0):
    f32 = jnp.float32
    bn, t = q.shape[:2]
    L = min(GLA_CHUNK, t)
    n = -(-t // L)
    pad = n * L - t

    def chunks(z):
        z = z.astype(f32)
        if pad:
            z = jnp.pad(z, ((0, 0), (0, pad), (0, 0), (0, 0)))
        return jnp.moveaxis(z.reshape(bn, n, L, *z.shape[2:]), 1, 0)

    qs = chunks(q) * (GLA_DK ** -0.5)
    kss, vs, las = chunks(k), chunks(v), chunks(log_a)
    causal = jnp.tril(jnp.ones((L, L), dtype=bool))[None, :, :, None, None]

    def step(s, inp):
        qb, kb, vb, ab = inp
        bc = jnp.cumsum(ab, axis=1)
        dec = jnp.exp(jnp.where(causal, bc[:, :, None] - bc[:, None, :], -jnp.inf))
        att = jnp.sum(qb[:, :, None] * kb[:, None, :] * dec, axis=-1)
        o = jnp.einsum('btsh,bshv->bthv', att, vb) + jnp.einsum('bthd,bhdv->bthv', qb * jnp.exp(bc), s)
        b_last = bc[:, -1]
        s = jnp.exp(b_last)[..., None] * s + jnp.einsum('bshd,bshv->bhdv', kb * jnp.exp(b_last[:, None] - bc), vb)
        return s, o

    s_t, os_ = lax.scan(step, s0, (qs, kss, vs, las))
    o = jnp.moveaxis(os_, 0, 1).reshape(bn, n * L, GLA_HEADS, GLA_DV)[:, :t]
    return o, s_t


def _layer(p, l, x, c, pos, kv_past, s5_re0, s5_im0, gla0):
    f32 = jnp.float32
    bn, t, _ = x.shape
    mod = (jax.nn.silu(c.astype(f32)) @ p['ada_w'][l].astype(f32) + p['ada_b'][l].astype(f32)).reshape(bn, N_SUB, 3, 1, D_MODEL)

    def pre(i, h):
        return _rmsnorm(h, p['norm_pre'][l, i]) * (1.0 + mod[:, i, 1]) + mod[:, i, 0]

    def post(i, y, w):
        return w * (1.0 + mod[:, i, 2]) * _rmsnorm(y, p['norm_post'][l, i])

    x = x + post(0, _swiglu(pre(0, x), p['ffn1_wi'][l], p['ffn1_wo'][l]), MACARON_W)

    proj = pre(1, x) @ p['w_in'][l]
    qa, ka, va, ub, qg, kg, vg, gg, rg = jnp.split(proj, _split_points(), axis=-1)

    q = _partial_rope(qa.reshape(bn, t, A_HEADS, 2, A_DH), pos)
    k = _partial_rope(ka.reshape(bn, t, A_HEADS, 2, A_DH), pos)
    v = va.reshape(bn, t, A_HEADS, 2 * A_DH)
    k_rows = k.reshape(bn, t, A_HEADS, 2 * A_DH)
    if kv_past is None:
        k_all, v_all, k_pos = k_rows, v, pos
    else:
        k_past, v_past = kv_past
        k_all = jnp.concatenate([k_past.astype(k_rows.dtype), k_rows], axis=1)
        v_all = jnp.concatenate([v_past.astype(v.dtype), v], axis=1)
        k_pos = jnp.arange(k_past.shape[1] + t, dtype=jnp.int32)
    lam_init = 0.8 - 0.6 * math.exp(-0.3 * l)
    lam = (jnp.exp(jnp.sum(p['lam_q1'][l].astype(f32) * p['lam_k1'][l].astype(f32)))
           - jnp.exp(jnp.sum(p['lam_q2'][l].astype(f32) * p['lam_k2'][l].astype(f32))) + lam_init)
    o_a = _diff_attention(q, k_all.reshape(bn, -1, A_HEADS, 2, A_DH), v_all, pos, k_pos, lam)
    o_a = (_rmsnorm(o_a, p['subln_g'][l]) * (1.0 - lam_init)).reshape(bn, t, A_WIDTH)

    o_b, s5_re, s5_im = _s5(ub, s5_re0, s5_im0, p['s5_lam_re'][l], p['s5_lam_im'][l], p['s5_b_re'][l],
                            p['s5_b_im'][l], p['s5_c_re'][l], p['s5_c_im'][l], p['s5_d'][l],
                            p['s5_log_dt'][l], p['s5_glu_w'][l], p['s5_glu_b'][l])

    log_a = jax.nn.log_sigmoid((rg @ p['gla_gate_w2'][l] + p['gla_gate_b'][l]).astype(f32)) / GLA_TAU
    s0 = jnp.zeros((bn, GLA_HEADS, GLA_DK, GLA_DV), f32) if gla0 is None else gla0.astype(f32)
    o_c, gla_new = _gla(qg.reshape(bn, t, GLA_HEADS, GLA_DK), kg.reshape(bn, t, GLA_HEADS, GLA_DK),
                        vg.reshape(bn, t, GLA_HEADS, GLA_DV), log_a.reshape(bn, t, GLA_HEADS, GLA_DK), s0)
    o_c = _rmsnorm(o_c, p['gla_norm_g'][l]).reshape(bn, t, C_WIDTH) * jax.nn.silu(gg.astype(f32))

    mixed = jnp.concatenate([o_a.astype(f32), o_b, o_c], axis=-1) @ p['w_out'][l]
    x = x + post(1, mixed, 1.0)

    x = x + post(2, _swiglu(pre(2, x), p['ffn2_wi'][l], p['ffn2_wo'][l]), MACARON_W)
    return x, (k_rows, v, s5_re, s5_im, gla_new)


def setup_inputs(seed: int = 0) -> dict:
    key = jax.random.key(seed)
    keys = iter(jax.random.split(key, 48))
    f32 = jnp.float32

    def nrm(shape, scale):
        return jax.random.normal(next(keys), shape, f32) * scale

    def gain(shape):
        return 1.0 + nrm(shape, 0.02)

    n_pages = PAST_LEN // PAGE_SIZE
    n_used = DEC_BATCH * n_pages
    n_phys = n_used + max(1, n_used // 4)
    perm = jax.random.permutation(next(keys), n_phys)
    page_table = perm[:n_used].reshape(DEC_BATCH, n_pages).astype(jnp.int32)
    kv_shape = (DEPTH, n_phys, PAGE_SIZE, A_HEADS, 2 * A_DH)
    s5_shape = (DEPTH, DEC_BATCH, S5_GROUPS, S5_P)
    return {
        'x_prompt': nrm((BATCH, SEQ, D_MODEL), 1.0),
        'x_sample': nrm((DEC_BATCH, DEC_SEQ, D_MODEL), 1.0),
        'cache_k': nrm(kv_shape, 1.0),
        'cache_v': nrm(kv_shape, 1.0),
        'state_s5_re': nrm(s5_shape, 0.5),
        'state_s5_im': nrm(s5_shape, 0.5),
        'state_gla': nrm((DEPTH, DEC_BATCH, GLA_HEADS, GLA_DK, GLA_DV), 1.0),
        'page_table': page_table,
        'c_prompt': nrm((BATCH, D_MODEL), 1.0),
        'c_sample': nrm((DEC_BATCH, D_MODEL), 1.0),
        'ada_w': nrm((DEPTH, D_MODEL, N_SUB * 3 * D_MODEL), 0.1 * D_MODEL ** -0.5),
        'ada_b': nrm((DEPTH, N_SUB * 3 * D_MODEL), 0.01),
        'norm_pre': gain((DEPTH, N_SUB, D_MODEL)),
        'norm_post': gain((DEPTH, N_SUB, D_MODEL)),
        'ffn1_wi': nrm((DEPTH, D_MODEL, 2 * D_FF), D_MODEL ** -0.5),
        'ffn1_wo': nrm((DEPTH, D_FF, D_MODEL), D_FF ** -0.5),
        'ffn2_wi': nrm((DEPTH, D_MODEL, 2 * D_FF), D_MODEL ** -0.5),
        'ffn2_wo': nrm((DEPTH, D_FF, D_MODEL), D_FF ** -0.5),
        'w_in': nrm((DEPTH, D_MODEL, IN_DIM), D_MODEL ** -0.5),
        'w_out': nrm((DEPTH, MIX, D_MODEL), MIX ** -0.5),
        'lam_q1': nrm((DEPTH, A_DH), 0.1),
        'lam_k1': nrm((DEPTH, A_DH), 0.1),
        'lam_q2': nrm((DEPTH, A_DH), 0.1),
        'lam_k2': nrm((DEPTH, A_DH), 0.1),
        'subln_g': gain((DEPTH, 2 * A_DH)),
        's5_lam_re': -0.5 + nrm((DEPTH, S5_GROUPS, S5_P), 0.01),
        's5_lam_im': jnp.pi * jnp.arange(S5_P, dtype=f32) + nrm((DEPTH, S5_GROUPS, S5_P), 0.01),
        's5_b_re': nrm((DEPTH, S5_GROUPS, S5_P, S5_GROUP), (2 * S5_GROUP) ** -0.5),
        's5_b_im': nrm((DEPTH, S5_GROUPS, S5_P, S5_GROUP), (2 * S5_GROUP) ** -0.5),
        's5_c_re': nrm((DEPTH, S5_GROUPS, S5_GROUP, S5_P), (2 * S5_P) ** -0.5),
        's5_c_im': nrm((DEPTH, S5_GROUPS, S5_GROUP, S5_P), (2 * S5_P) ** -0.5),
        's5_d': nrm((DEPTH, B_WIDTH), 1.0),
        's5_log_dt': jax.random.uniform(next(keys), (DEPTH, S5_GROUPS), f32, math.log(DT_MIN), math.log(DT_MAX)),
        's5_glu_w': nrm((DEPTH, B_WIDTH, B_WIDTH), B_WIDTH ** -0.5),
        's5_glu_b': nrm((DEPTH, B_WIDTH), 0.01),
        'gla_gate_w2': nrm((DEPTH, GLA_RANK, GLA_QK), GLA_RANK ** -0.5),
        'gla_gate_b': nrm((DEPTH, GLA_QK), 0.01),
        'gla_norm_g': gain((DEPTH, GLA_DV)),
    }


def reference(x_prompt, x_sample, cache_k, cache_v, state_s5_re, state_s5_im, state_gla, page_table,
              c_prompt, c_sample, ada_w, ada_b, norm_pre, norm_post, ffn1_wi, ffn1_wo, ffn2_wi, ffn2_wo,
              w_in, w_out, lam_q1, lam_k1, lam_q2, lam_k2, subln_g, s5_lam_re, s5_lam_im, s5_b_re, s5_b_im,
              s5_c_re, s5_c_im, s5_d, s5_log_dt, s5_glu_w, s5_glu_b, gla_gate_w2, gla_gate_b, gla_norm_g):
    p = dict(ada_w=ada_w, ada_b=ada_b, norm_pre=norm_pre, norm_post=norm_post, ffn1_wi=ffn1_wi,
             ffn1_wo=ffn1_wo, ffn2_wi=ffn2_wi, ffn2_wo=ffn2_wo, w_in=w_in, w_out=w_out, lam_q1=lam_q1,
             lam_k1=lam_k1, lam_q2=lam_q2, lam_k2=lam_k2, subln_g=subln_g, s5_lam_re=s5_lam_re,
             s5_lam_im=s5_lam_im, s5_b_re=s5_b_re, s5_b_im=s5_b_im, s5_c_re=s5_c_re, s5_c_im=s5_c_im,
             s5_d=s5_d, s5_log_dt=s5_log_dt, s5_glu_w=s5_glu_w, s5_glu_b=s5_glu_b,
             gla_gate_w2=gla_gate_w2, gla_gate_b=gla_gate_b, gla_norm_g=gla_norm_g)
    dec_b, n_pages = page_table.shape
    past_len = n_pages * cache_k.shape[2]
    pos_p = jnp.arange(x_prompt.shape[1], dtype=jnp.int32)
    pos_s = past_len + jnp.arange(x_sample.shape[1], dtype=jnp.int32)
    yp, ys = x_prompt, x_sample
    new_p, new_s = [], []
    for l in range(DEPTH):
        yp, st_p = _layer(p, l, yp, c_prompt, pos_p, None, None, None, None)
        k_past = cache_k[l, page_table].reshape(dec_b, past_len, A_HEADS, 2 * A_DH)
        v_past = cache_v[l, page_table].reshape(dec_b, past_len, A_HEADS, 2 * A_DH)
        ys, st_s = _layer(p, l, ys, c_sample, pos_s, (k_past, v_past), state_s5_re[l], state_s5_im[l], state_gla[l])
        new_p.append(st_p)
        new_s.append(st_s)
    kp, vp, srp, sip, gp = [jnp.stack(z) for z in zip(*new_p)]
    ks, vs, srs, sis, gs = [jnp.stack(z) for z in zip(*new_s)]
    return (yp.astype(x_prompt.dtype), ys.astype(x_sample.dtype), kp, vp, srp, sip, gp, ks, vs, srs, sis, gs)
```

```python
import functools
import math

import jax
import jax.numpy as jnp
import numpy as np
from jax import lax
from jax.experimental import pallas as pl
from jax.experimental.pallas import tpu as pltpu

F32 = jnp.float32
BF16 = jnp.bfloat16

A_HEADS = 4
A_DH = 64
A_HD = 2 * A_DH
A_WIDTH = A_HEADS * A_HD
ROT_DIM = 16
ROPE_THETA = 500000.0
NEG_INF = -1e30
S5_GROUPS = 16
S5_GROUP = 16
S5_P = 64
B_WIDTH = S5_GROUPS * S5_GROUP
S5_GP = S5_GROUPS * S5_P
GLA_HEADS = 4
GLA_DK = 32
GLA_DV = 64
GLA_QK = GLA_HEADS * GLA_DK
C_WIDTH = GLA_HEADS * GLA_DV
GLA_RANK = 16
GLA_TAU = 16.0
MACARON_W = 0.5
RMS_EPS = 1e-6
N_SUB = 3

V7X_LANES = 128
V7X_VMEM_LIMIT_BYTES = 56 * 1024 * 1024

ROW_TILE = 512
FFN_CHUNK = 256
ATTN_TILE = 512
S5_CHUNK_PROMPT = 8
GLA_BLOCK_PROMPT = 16
GLA_TILE = 512
DEC_PAD = 8


def _cparams(sem):
    return pltpu.CompilerParams(dimension_semantics=sem, vmem_limit_bytes=V7X_VMEM_LIMIT_BYTES)


def _resident(shape, index_map):
    return pl.BlockSpec(shape, index_map, pipeline_mode=pl.Buffered(1))


def _rms(x, g):
    return x * lax.rsqrt(jnp.mean(x * x, axis=-1, keepdims=True) + RMS_EPS) * g


def _sigmoid(x):
    return 1.0 / (1.0 + jnp.exp(-x))


def _group_of(idx, size):
    assert size & (size - 1) == 0
    return jnp.right_shift(idx, size.bit_length() - 1)


def _dot(a, b):
    return jnp.dot(a, b, preferred_element_type=F32)


def _dot_nt(a, b):
    return lax.dot_general(a, b, (((1,), (1,)), ((), ())), preferred_element_type=F32)


def _dot_tn(a, b):
    return lax.dot_general(a, b, (((0,), (0,)), ((), ())), preferred_element_type=F32)


def _ada_kernel(c_ref, w_ref, b_ref, o_ref):
    c = c_ref[...]
    a = (c * _sigmoid(c)).astype(BF16)
    o_ref[...] = _dot(a, w_ref[...].astype(BF16)) + b_ref[...]


def _ada_mod(c_all, ada_w, ada_b):
    depth, d, n = ada_w.shape
    rows = c_all.shape[0]
    tn = n // 4
    return pl.pallas_call(
        _ada_kernel,
        grid=(depth, n // tn),
        in_specs=[
            pl.BlockSpec((rows, d), lambda l, j: (0, 0)),
            pl.BlockSpec((None, d, tn), lambda l, j: (l, 0, j)),
            pl.BlockSpec((None, 1, tn), lambda l, j: (l, 0, j)),
        ],
        out_specs=pl.BlockSpec((None, rows, tn), lambda l, j: (l, 0, j)),
        out_shape=jax.ShapeDtypeStruct((depth, rows, n), F32),
        compiler_params=_cparams(("arbitrary", "arbitrary")),
        name="ada_mod",
    )(c_all, ada_w, ada_b.reshape(depth, 1, n))


class _Mod:
    def __init__(self, arr, layer, rows_per_seq, tm, d):
        self.arr = arr
        self.layer = layer
        self.d = d
        self.per_row = arr.ndim == 3
        self.tiles_per_seq = None if self.per_row else rows_per_seq // tm
        self.tm = tm

    def spec(self, sub, kind):
        col = N_SUB * sub + kind
        l = self.layer
        if self.per_row:
            return pl.BlockSpec((None, self.tm, self.d), lambda i: (l, i, col))
        tps = self.tiles_per_seq
        return pl.BlockSpec((None, None, 1, self.d), lambda i: (l, i // tps, 0, col))


def _ffn_kernel(x_ref, sh_ref, sc_ref, gt_ref, gpre_ref, gpost_ref, wi_ref, wo_ref, o_ref, *, d_ff, chunk):
    x = x_ref[...]
    h = (_rms(x, gpre_ref[...]) * (1.0 + sc_ref[...]) + sh_ref[...]).astype(BF16)
    acc = jnp.zeros(x.shape, F32)
    for c in range(d_ff // chunk):
        lo = c * chunk
        g = _dot(h, wi_ref[:, lo:lo + chunk])
        u = _dot(h, wi_ref[:, d_ff + lo:d_ff + lo + chunk])
        a = (g * _sigmoid(g) * u).astype(BF16)
        acc = acc + _dot(a, wo_ref[lo:lo + chunk, :])
    o_ref[...] = x + MACARON_W * (1.0 + gt_ref[...]) * _rms(acc, gpost_ref[...])


def _ffn(x, mod, sub, layer, norm_pre, norm_post, wi, wo, tm):
    n, d = x.shape
    d_ff = wo.shape[1]
    vec = pl.BlockSpec((None, None, 1, d), lambda i: (layer, sub, 0, 0))
    return pl.pallas_call(
        functools.partial(_ffn_kernel, d_ff=d_ff, chunk=FFN_CHUNK),
        grid=(n // tm,),
        in_specs=[
            pl.BlockSpec((tm, d), lambda i: (i, 0)),
            mod.spec(sub, 0), mod.spec(sub, 1), mod.spec(sub, 2),
            vec, vec,
            _resident((None, d, 2 * d_ff), lambda i: (layer, 0, 0)),
            _resident((None, d_ff, d), lambda i: (layer, 0, 0)),
        ],
        out_specs=pl.BlockSpec((tm, d), lambda i: (i, 0)),
        out_shape=jax.ShapeDtypeStruct((n, d), F32),
        compiler_params=_cparams(("parallel",)),
        name="ffn",
    )(x, mod.arr, mod.arr, mod.arr, norm_pre, norm_post, wi, wo)


def _inproj_kernel(x_ref, sh_ref, sc_ref, gpre_ref, w_ref, w2_ref, b2_ref, tc_ref, ts1_ref, ts2_ref,
                   qa_ref, qb_ref, k32_ref, kbf_ref, v32_ref, vbf_ref, u_ref, qkg_ref, vg_ref, gg_ref, bc_ref,
                   *, gla_block):
    x = x_ref[...]
    tm = x.shape[0]
    h = (_rms(x, gpre_ref[...]) * (1.0 + sc_ref[...]) + sh_ref[...]).astype(BF16)
    tc, ts1, ts2 = tc_ref[...], ts1_ref[...], ts2_ref[...]
    first_comp = lax.broadcasted_iota(jnp.int32, (tm, A_HD), 1) < A_DH

    def rope(z):
        up = pltpu.roll(z, A_HD - ROT_DIM // 2, axis=1)
        dn = pltpu.roll(z, ROT_DIM // 2, axis=1)
        return z * tc + up * ts1 + dn * ts2

    w = A_WIDTH
    for hd in range(A_HEADS):
        lo = hd * A_HD
        q = rope(_dot(h, w_ref[:, lo:lo + A_HD])) * (A_DH ** -0.5)
        qa_ref[:, lo:lo + A_HD] = jnp.where(first_comp, q, 0.0).astype(BF16)
        qb_ref[:, lo:lo + A_HD] = jnp.where(first_comp, 0.0, q).astype(BF16)
        k = rope(_dot(h, w_ref[:, w + lo:w + lo + A_HD]))
        k32_ref[:, lo:lo + A_HD] = k
        kbf_ref[:, lo:lo + A_HD] = k.astype(BF16)
    v = _dot(h, w_ref[:, 2 * w:3 * w])
    v32_ref[...] = v
    vbf_ref[...] = v.astype(BF16)
    off = 3 * w
    u_ref[...] = _dot(h, w_ref[:, off:off + B_WIDTH])
    off += B_WIDTH
    qkg_ref[...] = _dot(h, w_ref[:, off:off + 2 * GLA_QK])
    off += 2 * GLA_QK
    vg_ref[...] = _dot(h, w_ref[:, off:off + C_WIDTH])
    off += C_WIDTH
    gg_ref[...] = _dot(h, w_ref[:, off:off + C_WIDTH])
    off += C_WIDTH
    rg = _dot(h, w_ref[:, off:off + V7X_LANES]).astype(BF16)
    gate = _dot(rg, w2_ref[...]) + b2_ref[...]
    log_a = (jnp.minimum(gate, 0.0) - jnp.log(1.0 + jnp.exp(-jnp.abs(gate)))) / GLA_TAU
    r = lax.broadcasted_iota(jnp.int32, (tm, tm), 0)
    c = lax.broadcasted_iota(jnp.int32, (tm, tm), 1)
    same_block = _group_of(r, gla_block) == _group_of(c, gla_block)
    tri = jnp.where(same_block, jnp.where(c <= r, 1.0, 0.0), 0.0).astype(BF16)
    hi = log_a.astype(BF16)
    lo_part = (log_a - hi.astype(F32)).astype(BF16)
    bc_ref[...] = _dot(tri, hi) + _dot(tri, lo_part)


def _inproj(x, mod, layer, norm_pre, w_in, w2p, b2, tabs, rows_per_seq, tm, gla_block):
    n, d = x.shape
    n_cols = w_in.shape[2]
    tab_tiles = tabs[0].shape[0] // tm
    row = lambda width: pl.BlockSpec((tm, width), lambda i: (i, 0))
    tab = pl.BlockSpec((tm, V7X_LANES), lambda i: (i % tab_tiles, 0))
    outs = [
        (A_WIDTH, BF16), (A_WIDTH, BF16), (A_WIDTH, F32), (A_WIDTH, BF16), (A_WIDTH, F32), (A_WIDTH, BF16),
        (B_WIDTH, F32), (2 * GLA_QK, F32), (C_WIDTH, F32), (C_WIDTH, F32), (GLA_QK, F32),
    ]
    return pl.pallas_call(
        functools.partial(_inproj_kernel, gla_block=gla_block),
        grid=(n // tm,),
        in_specs=[
            row(d), mod.spec(1, 0), mod.spec(1, 1),
            pl.BlockSpec((None, None, 1, d), lambda i: (layer, 1, 0, 0)),
            _resident((None, d, n_cols), lambda i: (layer, 0, 0)),
            _resident((None, V7X_LANES, GLA_QK), lambda i: (layer, 0, 0)),
            pl.BlockSpec((None, 1, GLA_QK), lambda i: (layer, 0, 0)),
            tab, tab, tab,
        ],
        out_specs=[row(wd) for wd, _ in outs],
        out_shape=[jax.ShapeDtypeStruct((n, wd), dt) for wd, dt in outs],
        compiler_params=_cparams(("parallel",)),
        name="inproj",
    )(x, mod.arr, mod.arr, norm_pre, w_in, w2p, b2, *tabs)


def _rope_tables(pos):
    half = ROT_DIM // 2
    inv = ROPE_THETA ** (-jnp.arange(half, dtype=F32) / half)
    ang = pos.astype(F32)[:, None] * inv[None, :]
    cos, sin = jnp.cos(ang), jnp.sin(ang)
    lane = np.arange(V7X_LANES) % A_DH
    idx = lane % half
    is_lo = jnp.asarray(lane < half)
    is_hi = jnp.asarray((lane >= half) & (lane < ROT_DIM))
    cos_l, sin_l = cos[:, idx], sin[:, idx]
    tc = jnp.where(is_lo | is_hi, cos_l, 1.0)
    ts1 = jnp.where(is_lo, -sin_l, 0.0)
    ts2 = jnp.where(is_hi, sin_l, 0.0)
    return tc, ts1, ts2


def _attn_kernel(qi_tab, ki_tab, qa_ref, qb_ref, k_ref, v_ref, lam_ref, g_ref, o_ref, m_ref, l_ref, acc_ref):
    p = pl.program_id(2)
    qi = qi_tab[p]
    ki = ki_tab[p]
    tq = qa_ref.shape[0]

    @pl.when(ki == 0)
    def _():
        m_ref[...] = jnp.full(m_ref.shape, NEG_INF, F32)
        l_ref[...] = jnp.zeros(l_ref.shape, F32)
        acc_ref[...] = jnp.zeros(acc_ref.shape, F32)

    def step(masked):
        k = k_ref[...]
        v = v_ref[...]
        for c, q_ref in enumerate((qa_ref, qb_ref)):
            s = _dot_nt(q_ref[...], k)
            if masked:
                row = lax.broadcasted_iota(jnp.int32, s.shape, 0)
                col = lax.broadcasted_iota(jnp.int32, s.shape, 1)
                s = jnp.where(col <= row, s, NEG_INF)
            m_prev = m_ref[c]
            m_new = jnp.maximum(m_prev, jnp.max(s, axis=1, keepdims=True))
            alpha = jnp.exp(m_prev - m_new)
            pr = jnp.exp(s - m_new)
            l_ref[c] = alpha * l_ref[c] + jnp.sum(pr, axis=1, keepdims=True)
            acc_ref[c] = alpha * acc_ref[c] + _dot(pr.astype(BF16), v)
            m_ref[c] = m_new

    @pl.when(ki < qi)
    def _():
        step(False)

    @pl.when(ki == qi)
    def _():
        step(True)
        o = acc_ref[0] / l_ref[0] - lam_ref[...] * (acc_ref[1] / l_ref[1])
        o_ref[...] = _rms(o, g_ref[...]).astype(BF16)


def _attn_prompt(qa, qb, kbf, vbf, lam_vec, g_eff, batch, seq):
    t = min(ATTN_TILE, seq)
    nt = seq // t
    pairs = [(i, j) for i in range(nt) for j in range(i + 1)]
    qi_tab = jnp.asarray([a for a, _ in pairs], jnp.int32)
    ki_tab = jnp.asarray([b for _, b in pairs], jnp.int32)
    shp = (batch, seq, A_WIDTH)
    qspec = pl.BlockSpec((None, t, A_HD), lambda b, h, p, qt, kt: (b, qt[p], h))
    kspec = pl.BlockSpec((None, t, A_HD), lambda b, h, p, qt, kt: (b, kt[p], h))
    vec = pl.BlockSpec((1, A_HD), lambda b, h, p, qt, kt: (0, 0))
    out = pl.pallas_call(
        _attn_kernel,
        grid_spec=pltpu.PrefetchScalarGridSpec(
            num_scalar_prefetch=2,
            grid=(batch, A_HEADS, len(pairs)),
            in_specs=[qspec, qspec, kspec, kspec, vec, vec],
            out_specs=qspec,
            scratch_shapes=[
                pltpu.VMEM((2, t, 1), F32), pltpu.VMEM((2, t, 1), F32), pltpu.VMEM((2, t, A_HD), F32),
            ],
        ),
        out_shape=jax.ShapeDtypeStruct(shp, BF16),
        compiler_params=_cparams(("parallel", "parallel", "arbitrary")),
        name="attn_prompt",
    )(qi_tab, ki_tab, qa.reshape(shp), qb.reshape(shp), kbf.reshape(shp), vbf.reshape(shp), lam_vec, g_eff)
    return out.reshape(batch * seq, A_WIDTH)


def _attn_decode_kernel(pt_ref, *refs, n_pages, n_new):
    qa_ref, qb_ref, kn_ref, vn_ref, lam_ref, g_ref = refs[:6]
    kp = refs[6:6 + n_pages]
    vp = refs[6 + n_pages:6 + 2 * n_pages]
    o_ref = refs[6 + 2 * n_pages]
    del pt_ref
    rows = DEC_PAD
    head_of_lane = _group_of(lax.broadcasted_iota(jnp.int32, (rows, A_WIDTH), 1), A_HD)
    pieces = []
    for q_ref in (qa_ref, qb_ref):
        q = q_ref[...]
        for hd in range(A_HEADS):
            pieces.append(jnp.where(head_of_lane == hd, q, 0.0))
    qbd = jnp.concatenate(pieces, axis=0).astype(BF16)
    s = jnp.concatenate([_dot_nt(qbd, kp[j][...].astype(BF16)) for j in range(n_pages)], axis=1)
    s_new = _dot_nt(qbd, kn_ref[...].astype(BF16))
    nrows = 2 * A_HEADS * rows
    tok = jnp.bitwise_and(lax.broadcasted_iota(jnp.int32, (nrows, rows), 0), rows - 1)
    col = lax.broadcasted_iota(jnp.int32, (nrows, rows), 1)
    s_new = jnp.where(col <= jnp.minimum(tok, n_new - 1), s_new, NEG_INF)
    m = jnp.maximum(jnp.max(s, axis=1, keepdims=True), jnp.max(s_new, axis=1, keepdims=True))
    pr = jnp.exp(s - m)
    pr_new = jnp.exp(s_new - m)
    inv_l = 1.0 / (jnp.sum(pr, axis=1, keepdims=True) + jnp.sum(pr_new, axis=1, keepdims=True))
    pr = pr * inv_l
    pr_new = pr_new * inv_l
    half = nrows // 2
    lam = lam_ref[:, 0:1]
    wgt = (pr[:half] - lam * pr[half:]).astype(BF16)
    wgt_new = (pr_new[:half] - lam * pr_new[half:]).astype(BF16)
    page = kp[0].shape[0]
    o_all = _dot(wgt_new, vn_ref[...].astype(BF16))
    for j in range(n_pages):
        o_all = o_all + _dot(wgt[:, j * page:(j + 1) * page], vp[j][...].astype(BF16))
    g = g_ref[...]
    for hd in range(A_HEADS):
        oh = o_all[hd * rows:(hd + 1) * rows, hd * A_HD:(hd + 1) * A_HD]
        o_ref[:, hd * A_HD:(hd + 1) * A_HD] = _rms(oh, g)


def _attn_decode(qa8, qb8, kn8, vn8, cache_k, cache_v, page_table, layer, lam_vec, g_eff, n_new):
    nb = qa8.shape[0]
    n_pages = page_table.shape[1]
    page = cache_k.shape[2]
    tok_spec = pl.BlockSpec((None, DEC_PAD, A_WIDTH), lambda b, pt: (b, 0, 0))
    vec = pl.BlockSpec((1, A_HD), lambda b, pt: (0, 0))

    def page_spec(j):
        return pl.BlockSpec((None, None, page, A_WIDTH), lambda b, pt: (layer, pt[b, j], 0, 0))

    k_pages = [page_spec(j) for j in range(n_pages)]
    v_pages = [page_spec(j) for j in range(n_pages)]
    return pl.pallas_call(
        functools.partial(_attn_decode_kernel, n_pages=n_pages, n_new=n_new),
        grid_spec=pltpu.PrefetchScalarGridSpec(
            num_scalar_prefetch=1,
            grid=(nb,),
            in_specs=[tok_spec] * 4 + [vec, vec] + k_pages + v_pages,
            out_specs=tok_spec,
        ),
        out_shape=jax.ShapeDtypeStruct((nb, DEC_PAD, A_WIDTH), F32),
        compiler_params=_cparams(("arbitrary",)),
        name="attn_decode",
    )(page_table, qa8, qb8, kn8, vn8, lam_vec, g_eff, *([cache_k] * n_pages), *([cache_v] * n_pages))


def _mm_kernel(*refs, n_terms):
    o_ref = refs[-1]
    acc = None
    for i in range(n_terms):
        t = _dot(refs[2 * i][...].astype(BF16), refs[2 * i + 1][...])
        acc = t if acc is None else acc + t
    o_ref[...] = acc


def _mm(terms, tm=256, tn=1024):
    m = terms[0][0].shape[0]
    n = terms[0][1].shape[1]
    tm = min(tm, m)
    tn = min(tn, n)
    in_specs, args = [], []
    for x, w in terms:
        k = x.shape[1]
        in_specs += [pl.BlockSpec((tm, k), lambda i, j: (i, 0)), pl.BlockSpec((k, tn), lambda i, j: (0, j))]
        args += [x, w]
    return pl.pallas_call(
        functools.partial(_mm_kernel, n_terms=len(terms)),
        grid=(m // tm, n // tn),
        in_specs=in_specs,
        out_specs=pl.BlockSpec((tm, tn), lambda i, j: (i, j)),
        out_shape=jax.ShapeDtypeStruct((m, n), F32),
        compiler_params=_cparams(("parallel", "parallel")),
        name="s5_mm",
    )(*args)


def _s5_scan_kernel(e_ref, a_ref, xin_ref, xlast_ref):
    nc = e_ref.shape[0]
    gp = S5_GP
    are = a_ref[:, :gp]
    aim = a_ref[:, gp:]

    def body(j, carry):
        xr, xi = carry
        xin_ref[pl.ds(j, 1), :gp] = xr
        xin_ref[pl.ds(j, 1), gp:] = xi
        er = e_ref[pl.ds(j, 1), :gp]
        ei = e_ref[pl.ds(j, 1), gp:]
        return are * xr - aim * xi + er, are * xi + aim * xr + ei

    zero = jnp.zeros((1, gp), F32)
    xr, xi = lax.fori_loop(0, nc, body, (zero, zero))
    xlast_ref[:, :gp] = xr
    xlast_ref[:, gp:] = xi


def _s5_scan(e, a_chunk, batch):
    nc = e.shape[0] // batch
    w = e.shape[1]
    e3 = e.reshape(batch, nc, w)
    xin, xlast = pl.pallas_call(
        _s5_scan_kernel,
        grid=(batch,),
        in_specs=[pl.BlockSpec((None, nc, w), lambda b: (b, 0, 0)), pl.BlockSpec((1, w), lambda b: (0, 0))],
        out_specs=[pl.BlockSpec((None, nc, w), lambda b: (b, 0, 0)), pl.BlockSpec((None, 1, w), lambda b: (b, 0, 0))],
        out_shape=[jax.ShapeDtypeStruct((batch, nc, w), F32), jax.ShapeDtypeStruct((batch, 1, w), F32)],
        compiler_params=_cparams(("parallel",)),
        name="s5_scan",
    )(e3, a_chunk)
    return xin.reshape(batch * nc, w), xlast.reshape(batch, w)


def _s5_step_kernel(e_ref, a_ref, x0_ref, o_ref):
    gp = S5_GP
    are, aim = a_ref[:, :gp], a_ref[:, gp:]
    xr, xi = x0_ref[:, :gp], x0_ref[:, gp:]
    o_ref[:, :gp] = are * xr - aim * xi + e_ref[:, :gp]
    o_ref[:, gp:] = are * xi + aim * xr + e_ref[:, gp:]


def _s5_step(e, a_chunk, x0):
    return pl.pallas_call(
        _s5_step_kernel,
        out_shape=jax.ShapeDtypeStruct(e.shape, F32),
        name="s5_step",
    )(e, a_chunk, x0)


def _s5_matrices(lam_re, lam_im, b_re, b_im, c_re, c_im, log_dt, chunk):
    hp = lax.Precision.HIGHEST
    g, p, cc, L = S5_GROUPS, S5_P, S5_GROUP, chunk
    dt = jnp.exp(log_dt)[:, None]
    zr, zi = lam_re * dt, lam_im * dt
    d = jnp.arange(L + 1, dtype=F32)[:, None, None]
    mag = jnp.exp(zr[None] * d)
    pw_r, pw_i = mag * jnp.cos(zi[None] * d), mag * jnp.sin(zi[None] * d)
    ar, ai = pw_r[1] - 1.0, pw_i[1]
    den = lam_re * lam_re + lam_im * lam_im
    qr, qi = (ar * lam_re + ai * lam_im) / den, (ai * lam_re - ar * lam_im) / den
    bb_r = qr[..., None] * b_re - qi[..., None] * b_im
    bb_i = qr[..., None] * b_im + qi[..., None] * b_re
    eye = jnp.eye(g, dtype=F32)
    rev_r, rev_i = pw_r[L - 1::-1], pw_i[L - 1::-1]
    e_r = rev_r[..., None] * bb_r[None] - rev_i[..., None] * bb_i[None]
    e_i = rev_r[..., None] * bb_i[None] + rev_i[..., None] * bb_r[None]

    def bend_of(e):
        sgcp = jnp.transpose(e, (0, 1, 3, 2))
        return (sgcp[:, :, :, None, :] * eye[None, :, None, :, None]).reshape(L * g * cc, g * p)

    bend = jnp.concatenate([bend_of(e_r), bend_of(e_i)], axis=1)
    cp_r = c_re[None] * pw_r[:L, :, None, :] - c_im[None] * pw_i[:L, :, None, :]
    cp_i = c_re[None] * pw_i[:L, :, None, :] + c_im[None] * pw_r[:L, :, None, :]
    kd = (jnp.einsum('dgcp,gpe->dgce', cp_r, bb_r, precision=hp)
          - jnp.einsum('dgcp,gpe->dgce', cp_i, bb_i, precision=hp))
    s_idx = np.arange(L)[:, None]
    t_idx = np.arange(L)[None, :]
    lag = np.clip(t_idx - s_idx, 0, L - 1)
    causal = jnp.asarray((t_idx >= s_idx), F32)
    kst = kd[lag] * causal[:, :, None, None, None]
    sgetc = jnp.transpose(kst, (0, 2, 4, 1, 3))
    mtoep = (sgetc[:, :, :, :, None, :] * eye[None, :, None, None, :, None]).reshape(L * g * cc, L * g * cc)
    c1_r = c_re[None] * pw_r[1:, :, None, :] - c_im[None] * pw_i[1:, :, None, :]
    c1_i = c_re[None] * pw_i[1:, :, None, :] + c_im[None] * pw_r[1:, :, None, :]

    def ccar_of(c1):
        gptc = jnp.transpose(c1, (1, 3, 0, 2))
        return (gptc[:, :, :, None, :] * eye[:, None, None, :, None]).reshape(g * p, L * g * cc)

    ccar = jnp.concatenate([ccar_of(c1_r), ccar_of(-c1_i)], axis=0)
    a_chunk = jnp.concatenate([pw_r[L].reshape(1, g * p), pw_i[L].reshape(1, g * p)], axis=1)
    return bend.astype(BF16), mtoep.astype(BF16), ccar.astype(BF16), a_chunk


def _gla_kernel(*refs, lb, n_blocks, has_s0):
    if has_s0:
        qk_ref, v_ref, bc_ref, gg_ref, gn_ref, s0_ref, o_ref, s_out_ref, st_ref, obuf_ref = refs
    else:
        qk_ref, v_ref, bc_ref, gg_ref, gn_ref, o_ref, s_out_ref, st_ref, obuf_ref = refs
    t = pl.program_id(1)
    kw, vw = GLA_QK, C_WIDTH
    def same_head(shape, row_size, col_size):
        r = _group_of(lax.broadcasted_iota(jnp.int32, shape, 0), row_size)
        c = _group_of(lax.broadcasted_iota(jnp.int32, shape, 1), col_size)
        return r == c

    bd_mask = same_head((vw, kw), GLA_DV, GLA_DK)

    @pl.when(t == 0)
    def _():
        if has_s0:
            s0t = s0_ref[...].T
            st_ref[...] = jnp.where(bd_mask, jnp.concatenate([s0t] * GLA_HEADS, axis=0), 0.0)
        else:
            st_ref[...] = jnp.zeros(st_ref.shape, F32)

    head_expand = jnp.where(same_head((kw, vw), GLA_DK, GLA_DV), 1.0, 0.0).astype(BF16)
    row = lax.broadcasted_iota(jnp.int32, (lb, kw), 0)
    scale = GLA_DK ** -0.5

    def body(i, carry):
        r0 = pl.multiple_of(i * lb, lb)
        q = qk_ref[pl.ds(r0, lb), 0:kw] * scale
        k = qk_ref[pl.ds(r0, lb), kw:2 * kw]
        v = v_ref[pl.ds(r0, lb), :]
        bc = bc_ref[pl.ds(r0, lb), :]
        bl = bc[lb - 1:lb, :]
        st = st_ref[...]
        o = _dot_nt((q * jnp.exp(bc)).astype(BF16), st.astype(BF16))
        parts = []
        for s in range(lb):
            dec = jnp.exp(jnp.minimum(bc - bc[s:s + 1, :], 0.0))
            parts.append(jnp.where(row >= s, q * k[s:s + 1, :] * dec, 0.0))
        att = _dot(jnp.concatenate(parts, axis=0).astype(BF16), head_expand)
        for s in range(lb):
            o = o + att[s * lb:(s + 1) * lb, :] * v[s:s + 1, :]
        obuf_ref[pl.ds(r0, lb), :] = o
        ke = (k * jnp.exp(bl - bc)).astype(BF16)
        upd = _dot_tn(v.astype(BF16), ke)
        st_ref[...] = jnp.exp(bl) * st + jnp.where(bd_mask, upd, 0.0)
        return carry

    lax.fori_loop(0, n_blocks, body, 0)

    o = obuf_ref[...]
    seg_ones = jnp.where(same_head((vw, vw), GLA_DV, GLA_DV), 1.0, 0.0).astype(BF16)
    ms = _dot((o * o).astype(BF16), seg_ones) * (1.0 / GLA_DV)
    gg = gg_ref[...]
    o_ref[...] = (o * lax.rsqrt(ms + RMS_EPS) * gn_ref[...] * (gg * _sigmoid(gg))).astype(BF16)

    @pl.when(t == pl.num_programs(1) - 1)
    def _():
        st = st_ref[...]
        comp = st[0:GLA_DV]
        for hd in range(1, GLA_HEADS):
            comp = comp + st[hd * GLA_DV:(hd + 1) * GLA_DV]
        s_out_ref[...] = comp.T


def _gla(qkg, vg, bc, gg, gn_vec, s0, batch, seq, lb, tile):
    tile = min(tile, seq)
    nt = seq // tile
    row = lambda w: pl.BlockSpec((tile, w), lambda b, t: (b * nt + t, 0))
    st_spec = pl.BlockSpec((None, GLA_QK, GLA_DV), lambda b, t: (b, 0, 0))
    in_specs = [row(2 * GLA_QK), row(C_WIDTH), row(GLA_QK), row(C_WIDTH), pl.BlockSpec((1, C_WIDTH), lambda b, t: (0, 0))]
    args = [qkg, vg, bc, gg, gn_vec]
    if s0 is not None:
        in_specs.append(st_spec)
        args.append(s0)
    return pl.pallas_call(
        functools.partial(_gla_kernel, lb=lb, n_blocks=tile // lb, has_s0=s0 is not None),
        grid=(batch, nt),
        in_specs=in_specs,
        out_specs=[row(C_WIDTH), st_spec],
        out_shape=[jax.ShapeDtypeStruct((batch * seq, C_WIDTH), BF16),
                   jax.ShapeDtypeStruct((batch, GLA_QK, GLA_DV), F32)],
        scratch_shapes=[pltpu.VMEM((C_WIDTH, GLA_QK), F32), pltpu.VMEM((tile, C_WIDTH), F32)],
        compiler_params=_cparams(("parallel", "arbitrary")),
        name="gla",
    )(*args)


def _outproj_kernel(x_ref, oa_ref, y_ref, u_ref, d_ref, gw_ref, gb_ref, oc_ref, w_ref, gt_ref, gpost_ref, o_ref):
    y = y_ref[...] + d_ref[...] * u_ref[...]
    g = 0.5 * y * (1.0 + jnp.tanh(math.sqrt(2.0 / math.pi) * (y + 0.044715 * (y * y * y))))
    ob = g * _sigmoid(_dot(g.astype(BF16), gw_ref[...]) + gb_ref[...])
    a0, a1 = A_WIDTH, A_WIDTH + B_WIDTH
    mixed = (_dot(oa_ref[...], w_ref[0:a0, :]) + _dot(ob.astype(BF16), w_ref[a0:a1, :])
             + _dot(oc_ref[...], w_ref[a1:, :]))
    o_ref[...] = x_ref[...] + (1.0 + gt_ref[...]) * _rms(mixed, gpost_ref[...])


def _outproj(x, oa, y, u, s5_d, glu_w, glu_b, oc, w_out, mod, layer, norm_post, tm):
    n, d = x.shape
    row = lambda w: pl.BlockSpec((tm, w), lambda i: (i, 0))
    lvec = lambda w: pl.BlockSpec((None, 1, w), lambda i: (layer, 0, 0))
    return pl.pallas_call(
        _outproj_kernel,
        grid=(n // tm,),
        in_specs=[
            row(d), row(A_WIDTH), row(B_WIDTH), row(B_WIDTH), lvec(B_WIDTH),
            _resident((None, B_WIDTH, B_WIDTH), lambda i: (layer, 0, 0)), lvec(B_WIDTH),
            row(C_WIDTH), _resident((None, d, d), lambda i: (layer, 0, 0)),
            mod.spec(1, 2), pl.BlockSpec((None, None, 1, d), lambda i: (layer, 1, 0, 0)),
        ],
        out_specs=row(d),
        out_shape=jax.ShapeDtypeStruct((n, d), F32),
        compiler_params=_cparams(("parallel",)),
        name="outproj",
    )(x, oa, y, u, s5_d, glu_w, glu_b, oc, w_out, mod.arr, norm_post)


def kernel(x_prompt, x_sample, cache_k, cache_v, state_s5_re, state_s5_im, state_gla, page_table, c_prompt, c_sample, ada_w, ada_b, norm_pre, norm_post, ffn1_wi, ffn1_wo, ffn2_wi, ffn2_wo, w_in, w_out, lam_q1, lam_k1, lam_q2, lam_k2, subln_g, s5_lam_re, s5_lam_im, s5_b_re, s5_b_im, s5_c_re, s5_c_im, s5_d, s5_log_dt, s5_glu_w, s5_glu_b, gla_gate_w2, gla_gate_b, gla_norm_g):
    depth = ada_w.shape[0]
    batch, seq, d = x_prompt.shape
    dec_b, dec_t, _ = x_sample.shape
    n_pages = page_table.shape[1]
    page = cache_k.shape[2]
    past_len = n_pages * page
    n_p, n_s = batch * seq, dec_b * dec_t
    tm_p, tm_s = min(ROW_TILE, seq), min(ROW_TILE, n_s)

    n_c = batch + dec_b
    pad_c = (-n_c) % 8
    c_all = jnp.concatenate([c_prompt, c_sample, jnp.zeros((pad_c, d), F32)], axis=0)
    mod_all = _ada_mod(c_all, ada_w, ada_b)
    mod_p = mod_all[:, :batch].reshape(depth, batch, 1, N_SUB * 3 * d)
    mod_s = jnp.repeat(mod_all[:, batch:n_c], dec_t, axis=1)

    bf = lambda a: a.astype(BF16)
    wi1, wo1, wi2, wo2, w_out_b = bf(ffn1_wi), bf(ffn1_wo), bf(ffn2_wi), bf(ffn2_wo), bf(w_out)
    in_dim = w_in.shape[2]
    w_in_b = bf(jnp.pad(w_in, ((0, 0), (0, 0), (0, V7X_LANES - GLA_RANK))))
    assert w_in_b.shape[2] == in_dim - GLA_RANK + V7X_LANES
    w2p = bf(jnp.pad(gla_gate_w2, ((0, 0), (0, V7X_LANES - GLA_RANK), (0, 0))))
    b2 = gla_gate_b.reshape(depth, 1, GLA_QK)
    glu_w = bf(s5_glu_w)
    glu_b = s5_glu_b.reshape(depth, 1, B_WIDTH)
    s5_d3 = s5_d.reshape(depth, 1, B_WIDTH)
    npre = norm_pre.reshape(depth, N_SUB, 1, d)
    npost = norm_post.reshape(depth, N_SUB, 1, d)
    gn_vec = jnp.tile(gla_norm_g, (1, GLA_HEADS)).reshape(depth, 1, C_WIDTH)

    tabs_p = _rope_tables(jnp.arange(seq, dtype=jnp.int32))
    tabs_s = _rope_tables(past_len + jnp.arange(n_s, dtype=jnp.int32) % dec_t)

    cache_k4 = cache_k.reshape(depth, cache_k.shape[1], page, A_WIDTH)
    cache_v4 = cache_v.reshape(depth, cache_v.shape[1], page, A_WIDTH)

    def pad_tokens(a, edge=False):
        a3 = a.reshape(dec_b, dec_t, a.shape[-1])
        return jnp.pad(a3, ((0, 0), (0, DEC_PAD - dec_t), (0, 0)), mode='edge' if edge else 'constant')

    yp = x_prompt.reshape(n_p, d)
    ys = x_sample.reshape(n_s, d)
    outs_p, outs_s = [], []
    for l in range(depth):
        mp = _Mod(mod_p, l, seq, tm_p, d)
        ms = _Mod(mod_s, l, n_s, tm_s, d)
        lam_init = 0.8 - 0.6 * math.exp(-0.3 * l)
        lam = (jnp.exp(jnp.sum(lam_q1[l] * lam_k1[l])) - jnp.exp(jnp.sum(lam_q2[l] * lam_k2[l])) + lam_init)
        lam_vec = jnp.full((1, A_HD), lam, F32)
        g_eff = (subln_g[l] * (1.0 - lam_init)).reshape(1, A_HD)

        yp = _ffn(yp, mp, 0, l, npre, npost, wi1, wo1, tm_p)
        ys = _ffn(ys, ms, 0, l, npre, npost, wi1, wo1, tm_s)

        (qa_p, qb_p, k32_p, kbf_p, v32_p, vbf_p, u_p, qkg_p, vg_p, gg_p, bc_p) = _inproj(
            yp, mp, l, npre, w_in_b, w2p, b2, tabs_p, seq, tm_p, GLA_BLOCK_PROMPT)
        (qa_s, qb_s, k32_s, _, v32_s, _, u_s, qkg_s, vg_s, gg_s, bc_s) = _inproj(
            ys, ms, l, npre, w_in_b, w2p, b2, tabs_s, n_s, tm_s, dec_t)

        oa_p = _attn_prompt(qa_p, qb_p, kbf_p, vbf_p, lam_vec, g_eff, batch, seq)
        oa_s8 = _attn_decode(pad_tokens(qa_s.astype(F32)), pad_tokens(qb_s.astype(F32)), pad_tokens(k32_s),
                             pad_tokens(v32_s), cache_k4, cache_v4, page_table, l, lam_vec, g_eff, dec_t)
        oa_s = oa_s8[:, :dec_t].reshape(n_s, A_WIDTH).astype(BF16)

        s5p = (s5_lam_re[l], s5_lam_im[l], s5_b_re[l], s5_b_im[l], s5_c_re[l], s5_c_im[l], s5_log_dt[l])
        lp = min(S5_CHUNK_PROMPT, seq)
        bend, mtoep, ccar, a_chunk = _s5_matrices(*s5p, lp)
        uc_p = u_p.reshape(n_p // lp, lp * B_WIDTH)
        e_p = _mm([(uc_p, bend)])
        xin_p, xlast_p = _s5_scan(e_p, a_chunk, batch)
        y_p = _mm([(uc_p, mtoep), (xin_p, ccar)]).reshape(n_p, B_WIDTH)
        bend4, mtoep4, ccar4, a_chunk4 = _s5_matrices(*s5p, dec_t)
        uc_s = u_s.reshape(dec_b, dec_t * B_WIDTH)
        x0 = jnp.concatenate([state_s5_re[l].reshape(dec_b, S5_GP), state_s5_im[l].reshape(dec_b, S5_GP)], axis=1)
        xlast_s = _s5_step(_mm([(uc_s, bend4)]), a_chunk4, x0)
        y_s = _mm([(uc_s, mtoep4), (x0, ccar4)]).reshape(n_s, B_WIDTH)

        oc_p, gla_p = _gla(qkg_p, vg_p, bc_p, gg_p, gn_vec[l], None, batch, seq, GLA_BLOCK_PROMPT, GLA_TILE)
        flat8 = lambda a, edge=False: pad_tokens(a, edge).reshape(dec_b * DEC_PAD, a.shape[-1])
        oc_s8, gla_s = _gla(flat8(qkg_s), flat8(vg_s), flat8(bc_s, True), flat8(gg_s), gn_vec[l],
                            state_gla[l].reshape(dec_b, GLA_QK, GLA_DV), dec_b, DEC_PAD, DEC_PAD, DEC_PAD)
        oc_s = oc_s8.reshape(dec_b, DEC_PAD, C_WIDTH)[:, :dec_t].reshape(n_s, C_WIDTH)

        yp = _outproj(yp, oa_p, y_p, u_p, s5_d3, glu_w, glu_b, oc_p, w_out_b, mp, l, npost, tm_p)
        ys = _outproj(ys, oa_s, y_s, u_s, s5_d3, glu_w, glu_b, oc_s, w_out_b, ms, l, npost, tm_s)

        yp = _ffn(yp, mp, 2, l, npre, npost, wi2, wo2, tm_p)
        ys = _ffn(ys, ms, 2, l, npre, npost, wi2, wo2, tm_s)

        outs_p.append((k32_p.reshape(batch, seq, A_HEADS, A_HD), v32_p.reshape(batch, seq, A_HEADS, A_HD),
                       xlast_p[:, :S5_GP].reshape(batch, S5_GROUPS, S5_P),
                       xlast_p[:, S5_GP:].reshape(batch, S5_GROUPS, S5_P),
                       gla_p.reshape(batch, GLA_HEADS, GLA_DK, GLA_DV)))
        outs_s.append((k32_s.reshape(dec_b, dec_t, A_HEADS, A_HD), v32_s.reshape(dec_b, dec_t, A_HEADS, A_HD),
                       xlast_s[:, :S5_GP].reshape(dec_b, S5_GROUPS, S5_P),
                       xlast_s[:, S5_GP:].reshape(dec_b, S5_GROUPS, S5_P),
                       gla_s.reshape(dec_b, GLA_HEADS, GLA_DK, GLA_DV)))

    kp, vp, srp, sip, gp = [jnp.stack(z) for z in zip(*outs_p)]
    ks, vs, srs, sis, gs = [jnp.stack(z) for z in zip(*outs_s)]
    return (yp.reshape(batch, seq, d), ys.reshape(dec_b, dec_t, d), kp, vp, srp, sip, gp, ks, vs, srs, sis, gs)
```

```python
import functools
import math

import jax
import jax.numpy as jnp
import numpy as np
from jax import lax
from jax.experimental import pallas as pl
from jax.experimental.pallas import tpu as pltpu

F32 = jnp.float32
BF16 = jnp.bfloat16

A_HEADS = 4
A_DH = 64
A_HD = 2 * A_DH
A_WIDTH = A_HEADS * A_HD
ROT_DIM = 16
ROPE_THETA = 500000.0
NEG_INF = -1e30
LOG2_E = math.log2(math.e)
S5_GROUPS = 16
S5_GROUP = 16
S5_P = 64
B_WIDTH = S5_GROUPS * S5_GROUP
S5_GP = S5_GROUPS * S5_P
GLA_HEADS = 4
GLA_DK = 32
GLA_DV = 64
GLA_QK = GLA_HEADS * GLA_DK
C_WIDTH = GLA_HEADS * GLA_DV
GLA_RANK = 16
GLA_TAU = 16.0
MACARON_W = 0.5
RMS_EPS = 1e-6
N_SUB = 3

V7X_LANES = 128
V7X_VMEM_LIMIT_BYTES = 56 * 1024 * 1024

ROW_TILE = 512
FFN_TILE = 1024
FFN_CHUNK = 256
ATTN_TQ = 1024
ATTN_TK = 1024
S5_CHUNK_PROMPT = 8
GLA_BLOCK_PROMPT = 16
GLA_TILE = 512
DEC_PAD = 8


def _cparams(sem):
    return pltpu.CompilerParams(dimension_semantics=sem, vmem_limit_bytes=V7X_VMEM_LIMIT_BYTES)


def _resident(shape, index_map):
    return pl.BlockSpec(shape, index_map, pipeline_mode=pl.Buffered(1))


def _rms(x, g):
    return x * lax.rsqrt(jnp.mean(x * x, axis=-1, keepdims=True) + RMS_EPS) * g


def _sigmoid(x):
    return 1.0 / (1.0 + jnp.exp(-x))


def _group_of(idx, size):
    assert size & (size - 1) == 0
    return jnp.right_shift(idx, size.bit_length() - 1)


def _dot(a, b):
    return jnp.dot(a, b, preferred_element_type=F32)


def _dot_nt(a, b):
    return lax.dot_general(a, b, (((1,), (1,)), ((), ())), preferred_element_type=F32)


def _dot_tn(a, b):
    return lax.dot_general(a, b, (((0,), (0,)), ((), ())), preferred_element_type=F32)


def _ada_kernel(c_ref, w_ref, b_ref, o_ref):
    c = c_ref[...]
    a = (c * _sigmoid(c)).astype(BF16)
    o_ref[...] = _dot(a, w_ref[...].astype(BF16)) + b_ref[...]


def _ada_mod(c_all, ada_w, ada_b):
    depth, d, n = ada_w.shape
    rows = c_all.shape[0]
    tn = n // 4
    return pl.pallas_call(
        _ada_kernel,
        grid=(depth, n // tn),
        in_specs=[
            pl.BlockSpec((rows, d), lambda l, j: (0, 0)),
            pl.BlockSpec((None, d, tn), lambda l, j: (l, 0, j)),
            pl.BlockSpec((None, 1, tn), lambda l, j: (l, 0, j)),
        ],
        out_specs=pl.BlockSpec((None, rows, tn), lambda l, j: (l, 0, j)),
        out_shape=jax.ShapeDtypeStruct((depth, rows, n), F32),
        compiler_params=_cparams(("arbitrary", "arbitrary")),
        name="ada_mod",
    )(c_all, ada_w, ada_b.reshape(depth, 1, n))


class _Mod:
    def __init__(self, arr, layer, rows_per_seq, d):
        self.arr = arr
        self.layer = layer
        self.d = d
        self.per_row = arr.ndim == 3
        self.rows_per_seq = rows_per_seq

    def spec(self, sub, kind, tm):
        col = N_SUB * sub + kind
        l = self.layer
        if self.per_row:
            return pl.BlockSpec((None, tm, self.d), lambda i: (l, i, col))
        tps = self.rows_per_seq // tm
        return pl.BlockSpec((None, None, 1, self.d), lambda i: (l, i // tps, 0, col))


def _ffn_kernel(x_ref, sh_ref, sc_ref, gt_ref, gpre_ref, gpost_ref, wi_ref, wo_ref, o_ref, *, d_ff, chunk):
    x = x_ref[...]
    h = (_rms(x, gpre_ref[...]) * (1.0 + sc_ref[...]) + sh_ref[...]).astype(BF16)
    acc = jnp.zeros(x.shape, F32)
    for c in range(d_ff // chunk):
        lo = c * chunk
        g = _dot(h, wi_ref[:, lo:lo + chunk])
        u = _dot(h, wi_ref[:, d_ff + lo:d_ff + lo + chunk])
        a = (g * _sigmoid(g) * u).astype(BF16)
        acc = acc + _dot(a, wo_ref[lo:lo + chunk, :])
    o_ref[...] = x + MACARON_W * (1.0 + gt_ref[...]) * _rms(acc, gpost_ref[...])


def _ffn(x, mod, sub, layer, norm_pre, norm_post, wi, wo, tm):
    n, d = x.shape
    d_ff = wo.shape[1]
    vec = pl.BlockSpec((None, None, 1, d), lambda i: (layer, sub, 0, 0))
    return pl.pallas_call(
        functools.partial(_ffn_kernel, d_ff=d_ff, chunk=FFN_CHUNK),
        grid=(n // tm,),
        in_specs=[
            pl.BlockSpec((tm, d), lambda i: (i, 0)),
            mod.spec(sub, 0, tm), mod.spec(sub, 1, tm), mod.spec(sub, 2, tm),
            vec, vec,
            _resident((None, d, 2 * d_ff), lambda i: (layer, 0, 0)),
            _resident((None, d_ff, d), lambda i: (layer, 0, 0)),
        ],
        out_specs=pl.BlockSpec((tm, d), lambda i: (i, 0)),
        out_shape=jax.ShapeDtypeStruct((n, d), F32),
        compiler_params=_cparams(("parallel",)),
        name="ffn",
    )(x, mod.arr, mod.arr, mod.arr, norm_pre, norm_post, wi, wo)


def _inproj_kernel(x_ref, sh_ref, sc_ref, gpre_ref, w_ref, w2_ref, b2_ref, tc_ref, ts1_ref, ts2_ref, *refs,
                   gla_block, s5_chunk, transposed):
    out_refs, (ua_ref, ub_ref) = refs[:-2], refs[-2:]
    if transposed:
        qa_ref, qb_ref, k32_ref, kbf_ref, v32_ref, vt_ref, u_ref, uc_ref, qkg_ref, vg_ref, gg_ref, bc_ref = out_refs
    else:
        qa_ref, qb_ref, k32_ref, v32_ref, u_ref, uc_ref, qkg_ref, vg_ref, gg_ref, bc_ref = out_refs
    x = x_ref[...]
    tm = x.shape[0]
    h = (_rms(x, gpre_ref[...]) * (1.0 + sc_ref[...]) + sh_ref[...]).astype(BF16)
    tc, ts1, ts2 = tc_ref[...], ts1_ref[...], ts2_ref[...]
    if transposed:
        first_comp = lax.broadcasted_iota(jnp.int32, (A_HD, tm), 0) < A_DH
    else:
        first_comp = lax.broadcasted_iota(jnp.int32, (tm, A_HD), 1) < A_DH

    def rope(z):
        up = pltpu.roll(z, A_HD - ROT_DIM // 2, axis=1)
        dn = pltpu.roll(z, ROT_DIM // 2, axis=1)
        return z * tc + up * ts1 + dn * ts2

    w = A_WIDTH
    for hd in range(A_HEADS):
        lo = hd * A_HD
        q = rope(_dot(h, w_ref[:, lo:lo + A_HD])) * (A_DH ** -0.5 * LOG2_E)
        k = rope(_dot(h, w_ref[:, w + lo:w + lo + A_HD]))
        v = _dot(h, w_ref[:, 2 * w + lo:2 * w + lo + A_HD])
        k32_ref[pl.ds(hd, tm, stride=A_HEADS), :] = k
        v32_ref[pl.ds(hd, tm, stride=A_HEADS), :] = v
        if transposed:
            qt = q.T
            qa_ref[lo:lo + A_HD, :] = jnp.where(first_comp, qt, 0.0).astype(BF16)
            qb_ref[lo:lo + A_HD, :] = jnp.where(first_comp, 0.0, qt).astype(BF16)
            kbf_ref[:, lo:lo + A_HD] = k.astype(BF16)
            vt_ref[lo:lo + A_HD, :] = v.T.astype(BF16)
        else:
            qa_ref[:, lo:lo + A_HD] = jnp.where(first_comp, q, 0.0)
            qb_ref[:, lo:lo + A_HD] = jnp.where(first_comp, 0.0, q)
    off = 3 * w
    u = _dot(h, w_ref[:, off:off + B_WIDTH])
    u_ref[...] = u
    ua_ref[...] = u[:, :V7X_LANES]
    ub_ref[...] = u[:, V7X_LANES:]
    for s in range(s5_chunk):
        for half, part_ref in enumerate((ua_ref, ub_ref)):
            c0 = s * B_WIDTH + half * V7X_LANES
            uc_ref[:, c0:c0 + V7X_LANES] = part_ref[pl.ds(s, tm // s5_chunk, stride=s5_chunk), :]
    off += B_WIDTH
    qkg_ref[...] = _dot(h, w_ref[:, off:off + 2 * GLA_QK])
    off += 2 * GLA_QK
    vg_ref[...] = _dot(h, w_ref[:, off:off + C_WIDTH])
    off += C_WIDTH
    gg_ref[...] = _dot(h, w_ref[:, off:off + C_WIDTH])
    off += C_WIDTH
    rg = _dot(h, w_ref[:, off:off + V7X_LANES]).astype(BF16)
    gate = _dot(rg, w2_ref[...]) + b2_ref[...]
    log_a = (jnp.minimum(gate, 0.0) - jnp.log(1.0 + jnp.exp(-jnp.abs(gate)))) / GLA_TAU
    r = lax.broadcasted_iota(jnp.int32, (tm, tm), 0)
    c = lax.broadcasted_iota(jnp.int32, (tm, tm), 1)
    same_block = _group_of(r, gla_block) == _group_of(c, gla_block)
    tri = jnp.where(same_block, jnp.where(c <= r, 1.0, 0.0), 0.0).astype(BF16)
    hi = log_a.astype(BF16)
    lo_part = (log_a - hi.astype(F32)).astype(BF16)
    bc_ref[...] = _dot(tri, hi) + _dot(tri, lo_part)


def _inproj(x, mod, layer, norm_pre, w_in, w2p, b2, tabs, batch, tm, gla_block, s5_chunk, transposed):
    n, d = x.shape
    n_cols = w_in.shape[2]
    seq = n // batch
    tps = seq // tm if transposed else None
    tab_tiles = tabs[0].shape[0] // tm
    row = lambda width: pl.BlockSpec((tm, width), lambda i: (i, 0))
    tab = pl.BlockSpec((tm, V7X_LANES), lambda i: (i % tab_tiles, 0))
    f32_row = lambda width: (row(width), jax.ShapeDtypeStruct((n, width), F32))
    head_rows = (pl.BlockSpec((tm * A_HEADS, A_HD), lambda i: (i, 0)), jax.ShapeDtypeStruct((n * A_HEADS, A_HD), F32))
    chunk_rows = (pl.BlockSpec((tm // s5_chunk, s5_chunk * B_WIDTH), lambda i: (i, 0)),
                  jax.ShapeDtypeStruct((n // s5_chunk, s5_chunk * B_WIDTH), F32))
    if transposed:
        tr = (pl.BlockSpec((None, A_WIDTH, tm), lambda i: (i // tps, 0, i % tps)),
              jax.ShapeDtypeStruct((batch, A_WIDTH, seq), BF16))
        outs = [tr, tr, head_rows, (row(A_WIDTH), jax.ShapeDtypeStruct((n, A_WIDTH), BF16)), head_rows, tr]
    else:
        outs = [f32_row(A_WIDTH), f32_row(A_WIDTH), head_rows, head_rows]
    outs += [f32_row(B_WIDTH), chunk_rows, f32_row(2 * GLA_QK), f32_row(C_WIDTH), f32_row(C_WIDTH), f32_row(GLA_QK)]
    return pl.pallas_call(
        functools.partial(_inproj_kernel, gla_block=gla_block, s5_chunk=s5_chunk, transposed=transposed),
        scratch_shapes=[pltpu.VMEM((tm, V7X_LANES), F32), pltpu.VMEM((tm, V7X_LANES), F32)],
        grid=(n // tm,),
        in_specs=[
            row(d), mod.spec(1, 0, tm), mod.spec(1, 1, tm),
            pl.BlockSpec((None, None, 1, d), lambda i: (layer, 1, 0, 0)),
            _resident((None, d, n_cols), lambda i: (layer, 0, 0)),
            _resident((None, V7X_LANES, GLA_QK), lambda i: (layer, 0, 0)),
            pl.BlockSpec((None, 1, GLA_QK), lambda i: (layer, 0, 0)),
            tab, tab, tab,
        ],
        out_specs=[spec for spec, _ in outs],
        out_shape=[shape for _, shape in outs],
        compiler_params=_cparams(("parallel",)),
        name="inproj",
    )(x, mod.arr, mod.arr, norm_pre, w_in, w2p, b2, *tabs)


def _rope_tables(pos):
    half = ROT_DIM // 2
    inv = ROPE_THETA ** (-jnp.arange(half, dtype=F32) / half)
    ang = pos.astype(F32)[:, None] * inv[None, :]
    cos, sin = jnp.cos(ang), jnp.sin(ang)
    lane = np.arange(V7X_LANES) % A_DH
    idx = lane % half
    is_lo = jnp.asarray(lane < half)
    is_hi = jnp.asarray((lane >= half) & (lane < ROT_DIM))
    cos_l, sin_l = cos[:, idx], sin[:, idx]
    tc = jnp.where(is_lo | is_hi, cos_l, 1.0)
    ts1 = jnp.where(is_lo, -sin_l, 0.0)
    ts2 = jnp.where(is_hi, sin_l, 0.0)
    return tc, ts1, ts2


def _attn_kernel(qi_tab, ki_tab, qa_ref, qb_ref, k_ref, vt_ref, lam_ref, g_ref, o_ref, m_ref, l_ref, acc_ref,
                 *, tq, tk):
    p = pl.program_id(2)
    qi = qi_tab[p]
    ki = ki_tab[p]
    q0 = qi * tq
    k0 = ki * tk

    @pl.when(ki == 0)
    def _():
        m_ref[...] = jnp.full(m_ref.shape, NEG_INF, F32)
        l_ref[...] = jnp.zeros(l_ref.shape, F32)
        acc_ref[...] = jnp.zeros(acc_ref.shape, F32)

    def step(masked):
        k = k_ref[...]
        vt = vt_ref[...]
        for c, q_ref in enumerate((qa_ref, qb_ref)):
            s = _dot(k, q_ref[...])
            if masked:
                key = k0 + lax.broadcasted_iota(jnp.int32, s.shape, 0)
                qry = q0 + lax.broadcasted_iota(jnp.int32, s.shape, 1)
                s = jnp.where(key <= qry, s, NEG_INF)
            m_prev = m_ref[c]
            m_new = jnp.maximum(m_prev, jnp.max(s, axis=0, keepdims=True))
            alpha = jnp.exp2(m_prev - m_new)
            pr = jnp.exp2(s - m_new)
            l_ref[c] = alpha * l_ref[c] + jnp.sum(pr, axis=0, keepdims=True)
            acc_ref[c] = alpha * acc_ref[c] + _dot(vt, pr.astype(BF16))
            m_ref[c] = m_new

    fully_visible = k0 + (tk - 1) <= q0

    @pl.when(fully_visible)
    def _():
        step(False)

    @pl.when(jnp.logical_not(fully_visible))
    def _():
        step(True)

    @pl.when(k0 + tk >= q0 + tq)
    def _():
        ot = acc_ref[0] / l_ref[0] - lam_ref[:, 0:1] * (acc_ref[1] / l_ref[1])
        ms = jnp.mean(ot * ot, axis=0, keepdims=True)
        o_ref[...] = ((ot * lax.rsqrt(ms + RMS_EPS)).T * g_ref[...]).astype(BF16)


def _attn_prompt(qat, qbt, kbf, vt, lam_vec, g_eff, batch, seq):
    tq = min(ATTN_TQ, seq)
    tk = min(ATTN_TK, seq)
    assert tq % tk == 0
    pairs = [(i, j) for i in range(seq // tq) for j in range(((i + 1) * tq) // tk)]
    qi_tab = jnp.asarray([a for a, _ in pairs], jnp.int32)
    ki_tab = jnp.asarray([b for _, b in pairs], jnp.int32)
    qspec = pl.BlockSpec((None, A_HD, tq), lambda b, h, p, qt, kt: (b, h, qt[p]))
    kspec = pl.BlockSpec((None, tk, A_HD), lambda b, h, p, qt, kt: (b, kt[p], h))
    vspec = pl.BlockSpec((None, A_HD, tk), lambda b, h, p, qt, kt: (b, h, kt[p]))
    ospec = pl.BlockSpec((None, tq, A_HD), lambda b, h, p, qt, kt: (b, qt[p], h))
    vec = pl.BlockSpec((1, A_HD), lambda b, h, p, qt, kt: (0, 0))
    out = pl.pallas_call(
        functools.partial(_attn_kernel, tq=tq, tk=tk),
        grid_spec=pltpu.PrefetchScalarGridSpec(
            num_scalar_prefetch=2,
            grid=(batch, A_HEADS, len(pairs)),
            in_specs=[qspec, qspec, kspec, vspec, vec, vec],
            out_specs=ospec,
            scratch_shapes=[
                pltpu.VMEM((2, 1, tq), F32), pltpu.VMEM((2, 1, tq), F32), pltpu.VMEM((2, A_HD, tq), F32),
            ],
        ),
        out_shape=jax.ShapeDtypeStruct((batch, seq, A_WIDTH), BF16),
        compiler_params=_cparams(("parallel", "parallel", "arbitrary")),
        name="attn_prompt",
    )(qi_tab, ki_tab, qat, qbt, kbf.reshape(batch, seq, A_WIDTH), vt, lam_vec, g_eff)
    return out.reshape(batch * seq, A_WIDTH)


def _attn_decode_kernel(pt_ref, *refs, n_pages, n_new):
    qa_ref, qb_ref, kn_ref, vn_ref, lam_ref, g_ref = refs[:6]
    kp = refs[6:6 + n_pages]
    vp = refs[6 + n_pages:6 + 2 * n_pages]
    o_ref = refs[6 + 2 * n_pages]
    del pt_ref
    rows = DEC_PAD
    grp = 2 * rows
    nq = A_HEADS * grp
    page_rows = kp[0].shape[0]
    head_lanes = [slice(hd * A_HD, (hd + 1) * A_HD) for hd in range(A_HEADS)]
    q_all = jnp.concatenate([r[:, lanes] for lanes in head_lanes for r in (qa_ref, qb_ref)], axis=0).astype(BF16)
    s = jnp.concatenate([_dot_nt(q_all, kp[j][...].astype(BF16)) for j in range(n_pages)], axis=1)
    row_head = _group_of(lax.broadcasted_iota(jnp.int32, s.shape, 0), grp)
    col_head = jnp.bitwise_and(lax.broadcasted_iota(jnp.int32, s.shape, 1), A_HEADS - 1)
    s = jnp.where(row_head == col_head, s, NEG_INF)
    s_new = _dot_nt(q_all, kn_ref[...].astype(BF16))
    r_new = lax.broadcasted_iota(jnp.int32, s_new.shape, 0)
    c_new = lax.broadcasted_iota(jnp.int32, s_new.shape, 1)
    tok = jnp.bitwise_and(r_new, rows - 1)
    visible = _group_of(c_new, A_HEADS) <= jnp.minimum(tok, n_new - 1)
    same_head = _group_of(r_new, grp) == jnp.bitwise_and(c_new, A_HEADS - 1)
    s_new = jnp.where(same_head, jnp.where(visible, s_new, NEG_INF), NEG_INF)
    m = jnp.maximum(jnp.max(s, axis=1, keepdims=True), jnp.max(s_new, axis=1, keepdims=True))
    pr = jnp.exp2(s - m)
    pr_new = jnp.exp2(s_new - m)
    inv_l = 1.0 / (jnp.sum(pr, axis=1, keepdims=True) + jnp.sum(pr_new, axis=1, keepdims=True))
    pr = pr * inv_l
    pr_new = pr_new * inv_l
    lam = lam_ref[:, 0:1]

    def diff(p):
        return jnp.concatenate([p[hd * grp:hd * grp + rows] - lam * p[hd * grp + rows:(hd + 1) * grp]
                                for hd in range(A_HEADS)], axis=0).astype(BF16)

    wgt = diff(pr)
    wgt_new = diff(pr_new)
    o = _dot(wgt_new, vn_ref[...].astype(BF16))
    for j in range(n_pages):
        o = o + _dot(wgt[:, j * page_rows:(j + 1) * page_rows], vp[j][...].astype(BF16))
    g = g_ref[...]
    for hd, lanes in enumerate(head_lanes):
        o_ref[:, lanes] = _rms(o[hd * rows:(hd + 1) * rows], g)


def _attn_decode(qa8, qb8, kn8, vn8, cache_k, cache_v, page_table, layer, lam_vec, g_eff, n_new):
    nb = qa8.shape[0]
    n_pages = page_table.shape[1]
    page_rows = cache_k.shape[2]
    tok_spec = pl.BlockSpec((None, DEC_PAD, A_WIDTH), lambda b, pt: (b, 0, 0))
    new_spec = pl.BlockSpec((None, DEC_PAD * A_HEADS, A_HD), lambda b, pt: (b, 0, 0))
    vec = pl.BlockSpec((1, A_HD), lambda b, pt: (0, 0))

    def page_spec(j):
        return pl.BlockSpec((None, None, page_rows, A_HD), lambda b, pt: (layer, pt[b, j], 0, 0))

    k_pages = [page_spec(j) for j in range(n_pages)]
    v_pages = [page_spec(j) for j in range(n_pages)]
    return pl.pallas_call(
        functools.partial(_attn_decode_kernel, n_pages=n_pages, n_new=n_new),
        grid_spec=pltpu.PrefetchScalarGridSpec(
            num_scalar_prefetch=1,
            grid=(nb,),
            in_specs=[tok_spec, tok_spec, new_spec, new_spec, vec, vec] + k_pages + v_pages,
            out_specs=tok_spec,
        ),
        out_shape=jax.ShapeDtypeStruct((nb, DEC_PAD, A_WIDTH), F32),
        compiler_params=_cparams(("arbitrary",)),
        name="attn_decode",
    )(page_table, qa8, qb8, kn8, vn8, lam_vec, g_eff, *([cache_k] * n_pages), *([cache_v] * n_pages))


def _mm_kernel(*refs, n_terms):
    o_ref = refs[-1]
    acc = None
    for i in range(n_terms):
        t = _dot(refs[2 * i][...].astype(BF16), refs[2 * i + 1][...])
        acc = t if acc is None else acc + t
    o_ref[...] = acc


def _mm(terms, tm=256, tn=1024):
    m = terms[0][0].shape[0]
    n = terms[0][1].shape[1]
    tm = min(tm, m)
    tn = min(tn, n)
    in_specs, args = [], []
    for x, w in terms:
        k = x.shape[1]
        in_specs += [pl.BlockSpec((tm, k), lambda i, j: (i, 0)), pl.BlockSpec((k, tn), lambda i, j: (0, j))]
        args += [x, w]
    return pl.pallas_call(
        functools.partial(_mm_kernel, n_terms=len(terms)),
        grid=(m // tm, n // tn),
        in_specs=in_specs,
        out_specs=pl.BlockSpec((tm, tn), lambda i, j: (i, j)),
        out_shape=jax.ShapeDtypeStruct((m, n), F32),
        compiler_params=_cparams(("parallel", "parallel")),
        name="s5_mm",
    )(*args)


def _s5_scan_kernel(e_ref, a_ref, xin_ref, xlast_ref):
    nc = e_ref.shape[0]
    gp = S5_GP
    are = a_ref[:, :gp]
    aim = a_ref[:, gp:]

    def body(j, carry):
        xr, xi = carry
        xin_ref[pl.ds(j, 1), :gp] = xr
        xin_ref[pl.ds(j, 1), gp:] = xi
        er = e_ref[pl.ds(j, 1), :gp]
        ei = e_ref[pl.ds(j, 1), gp:]
        return are * xr - aim * xi + er, are * xi + aim * xr + ei

    zero = jnp.zeros((1, gp), F32)
    xr, xi = lax.fori_loop(0, nc, body, (zero, zero))
    xlast_ref[:, :gp] = xr
    xlast_ref[:, gp:] = xi


def _s5_scan(e, a_chunk, batch):
    nc = e.shape[0] // batch
    w = e.shape[1]
    e3 = e.reshape(batch, nc, w)
    xin, xlast = pl.pallas_call(
        _s5_scan_kernel,
        grid=(batch,),
        in_specs=[pl.BlockSpec((None, nc, w), lambda b: (b, 0, 0)), pl.BlockSpec((1, w), lambda b: (0, 0))],
        out_specs=[pl.BlockSpec((None, nc, w), lambda b: (b, 0, 0)), pl.BlockSpec((None, 1, w), lambda b: (b, 0, 0))],
        out_shape=[jax.ShapeDtypeStruct((batch, nc, w), F32), jax.ShapeDtypeStruct((batch, 1, w), F32)],
        compiler_params=_cparams(("parallel",)),
        name="s5_scan",
    )(e3, a_chunk)
    return xin.reshape(batch * nc, w), xlast.reshape(batch, w)


def _s5_step_kernel(e_ref, a_ref, x0_ref, o_ref):
    gp = S5_GP
    are, aim = a_ref[:, :gp], a_ref[:, gp:]
    xr, xi = x0_ref[:, :gp], x0_ref[:, gp:]
    o_ref[:, :gp] = are * xr - aim * xi + e_ref[:, :gp]
    o_ref[:, gp:] = are * xi + aim * xr + e_ref[:, gp:]


def _s5_step(e, a_chunk, x0):
    return pl.pallas_call(
        _s5_step_kernel,
        out_shape=jax.ShapeDtypeStruct(e.shape, F32),
        name="s5_step",
    )(e, a_chunk, x0)


def _s5_matrices(lam_re, lam_im, b_re, b_im, c_re, c_im, log_dt, chunk):
    hp = lax.Precision.HIGHEST
    g, p, cc, L = S5_GROUPS, S5_P, S5_GROUP, chunk
    dt = jnp.exp(log_dt)[:, None]
    zr, zi = lam_re * dt, lam_im * dt
    d = jnp.arange(L + 1, dtype=F32)[:, None, None]
    mag = jnp.exp(zr[None] * d)
    pw_r, pw_i = mag * jnp.cos(zi[None] * d), mag * jnp.sin(zi[None] * d)
    ar, ai = pw_r[1] - 1.0, pw_i[1]
    den = lam_re * lam_re + lam_im * lam_im
    qr, qi = (ar * lam_re + ai * lam_im) / den, (ai * lam_re - ar * lam_im) / den
    bb_r = qr[..., None] * b_re - qi[..., None] * b_im
    bb_i = qr[..., None] * b_im + qi[..., None] * b_re
    eye = jnp.eye(g, dtype=F32)
    rev_r, rev_i = pw_r[L - 1::-1], pw_i[L - 1::-1]
    e_r = rev_r[..., None] * bb_r[None] - rev_i[..., None] * bb_i[None]
    e_i = rev_r[..., None] * bb_i[None] + rev_i[..., None] * bb_r[None]

    def bend_of(e):
        sgcp = jnp.transpose(e, (0, 1, 3, 2))
        return (sgcp[:, :, :, None, :] * eye[None, :, None, :, None]).reshape(L * g * cc, g * p)

    bend = jnp.concatenate([bend_of(e_r), bend_of(e_i)], axis=1)
    cp_r = c_re[None] * pw_r[:L, :, None, :] - c_im[None] * pw_i[:L, :, None, :]
    cp_i = c_re[None] * pw_i[:L, :, None, :] + c_im[None] * pw_r[:L, :, None, :]
    kd = (jnp.einsum('dgcp,gpe->dgce', cp_r, bb_r, precision=hp)
          - jnp.einsum('dgcp,gpe->dgce', cp_i, bb_i, precision=hp))
    s_idx = np.arange(L)[:, None]
    t_idx = np.arange(L)[None, :]
    lag = np.clip(t_idx - s_idx, 0, L - 1)
    causal = jnp.asarray((t_idx >= s_idx), F32)
    kst = kd[lag] * causal[:, :, None, None, None]
    sgetc = jnp.transpose(kst, (0, 2, 4, 1, 3))
    mtoep = (sgetc[:, :, :, :, None, :] * eye[None, :, None, None, :, None]).reshape(L * g * cc, L * g * cc)
    c1_r = c_re[None] * pw_r[1:, :, None, :] - c_im[None] * pw_i[1:, :, None, :]
    c1_i = c_re[None] * pw_i[1:, :, None, :] + c_im[None] * pw_r[1:, :, None, :]

    def ccar_of(c1):
        gptc = jnp.transpose(c1, (1, 3, 0, 2))
        return (gptc[:, :, :, None, :] * eye[:, None, None, :, None]).reshape(g * p, L * g * cc)

    ccar = jnp.concatenate([ccar_of(c1_r), ccar_of(-c1_i)], axis=0)
    a_chunk = jnp.concatenate([pw_r[L].reshape(1, g * p), pw_i[L].reshape(1, g * p)], axis=1)
    return bend.astype(BF16), mtoep.astype(BF16), ccar.astype(BF16), a_chunk


def _gla_kernel(*refs, lb, n_blocks, has_s0):
    if has_s0:
        qk_ref, v_ref, bc_ref, gg_ref, gn_ref, s0_ref, o_ref, s_out_ref, st_ref, obuf_ref = refs
    else:
        qk_ref, v_ref, bc_ref, gg_ref, gn_ref, o_ref, s_out_ref, st_ref, obuf_ref = refs
    t = pl.program_id(1)
    kw, vw = GLA_QK, C_WIDTH
    def same_head(shape, row_size, col_size):
        r = _group_of(lax.broadcasted_iota(jnp.int32, shape, 0), row_size)
        c = _group_of(lax.broadcasted_iota(jnp.int32, shape, 1), col_size)
        return r == c

    bd_mask = same_head((vw, kw), GLA_DV, GLA_DK)

    @pl.when(t == 0)
    def _():
        if has_s0:
            s0t = s0_ref[...].T
            st_ref[...] = jnp.where(bd_mask, jnp.concatenate([s0t] * GLA_HEADS, axis=0), 0.0)
        else:
            st_ref[...] = jnp.zeros(st_ref.shape, F32)

    head_expand = jnp.where(same_head((kw, vw), GLA_DK, GLA_DV), 1.0, 0.0).astype(BF16)
    row = lax.broadcasted_iota(jnp.int32, (lb, kw), 0)
    scale = GLA_DK ** -0.5

    def body(i, carry):
        r0 = pl.multiple_of(i * lb, lb)
        q = qk_ref[pl.ds(r0, lb), 0:kw] * scale
        k = qk_ref[pl.ds(r0, lb), kw:2 * kw]
        v = v_ref[pl.ds(r0, lb), :]
        bc = bc_ref[pl.ds(r0, lb), :]
        bl = bc[lb - 1:lb, :]
        st = st_ref[...]
        o = _dot_nt((q * jnp.exp(bc)).astype(BF16), st.astype(BF16))
        parts = []
        for s in range(lb):
            dec = jnp.exp(jnp.minimum(bc - bc[s:s + 1, :], 0.0))
            parts.append(jnp.where(row >= s, q * k[s:s + 1, :] * dec, 0.0))
        att = _dot(jnp.concatenate(parts, axis=0).astype(BF16), head_expand)
        for s in range(lb):
            o = o + att[s * lb:(s + 1) * lb, :] * v[s:s + 1, :]
        obuf_ref[pl.ds(r0, lb), :] = o
        ke = (k * jnp.exp(bl - bc)).astype(BF16)
        upd = _dot_tn(v.astype(BF16), ke)
        st_ref[...] = jnp.exp(bl) * st + jnp.where(bd_mask, upd, 0.0)
        return carry

    lax.fori_loop(0, n_blocks, body, 0)

    o = obuf_ref[...]
    seg_ones = jnp.where(same_head((vw, vw), GLA_DV, GLA_DV), 1.0, 0.0).astype(BF16)
    ms = _dot((o * o).astype(BF16), seg_ones) * (1.0 / GLA_DV)
    gg = gg_ref[...]
    o_ref[...] = (o * lax.rsqrt(ms + RMS_EPS) * gn_ref[...] * (gg * _sigmoid(gg))).astype(BF16)

    @pl.when(t == pl.num_programs(1) - 1)
    def _():
        st = st_ref[...]
        comp = st[0:GLA_DV]
        for hd in range(1, GLA_HEADS):
            comp = comp + st[hd * GLA_DV:(hd + 1) * GLA_DV]
        s_out_ref[...] = comp.T


def _gla(qkg, vg, bc, gg, gn_vec, s0, batch, seq, lb, tile):
    tile = min(tile, seq)
    nt = seq // tile
    row = lambda w: pl.BlockSpec((tile, w), lambda b, t: (b * nt + t, 0))
    st_spec = pl.BlockSpec((None, GLA_QK, GLA_DV), lambda b, t: (b, 0, 0))
    in_specs = [row(2 * GLA_QK), row(C_WIDTH), row(GLA_QK), row(C_WIDTH), pl.BlockSpec((1, C_WIDTH), lambda b, t: (0, 0))]
    args = [qkg, vg, bc, gg, gn_vec]
    if s0 is not None:
        in_specs.append(st_spec)
        args.append(s0)
    return pl.pallas_call(
        functools.partial(_gla_kernel, lb=lb, n_blocks=tile // lb, has_s0=s0 is not None),
        grid=(batch, nt),
        in_specs=in_specs,
        out_specs=[row(C_WIDTH), st_spec],
        out_shape=[jax.ShapeDtypeStruct((batch * seq, C_WIDTH), BF16),
                   jax.ShapeDtypeStruct((batch, GLA_QK, GLA_DV), F32)],
        scratch_shapes=[pltpu.VMEM((C_WIDTH, GLA_QK), F32), pltpu.VMEM((tile, C_WIDTH), F32)],
        compiler_params=_cparams(("parallel", "arbitrary")),
        name="gla",
    )(*args)


def _outproj_kernel(x_ref, oa_ref, yc_ref, u_ref, d_ref, gw_ref, gb_ref, oc_ref, w_ref, gt_ref, gpost_ref, o_ref,
                    ya_ref, yb_ref, *, s5_chunk):
    tm = x_ref.shape[0]
    for s in range(s5_chunk):
        for half, part_ref in enumerate((ya_ref, yb_ref)):
            c0 = s * B_WIDTH + half * V7X_LANES
            part_ref[pl.ds(s, tm // s5_chunk, stride=s5_chunk), :] = yc_ref[:, c0:c0 + V7X_LANES]
    y = jnp.concatenate([ya_ref[...], yb_ref[...]], axis=1) + d_ref[...] * u_ref[...]
    g = 0.5 * y * (1.0 + jnp.tanh(math.sqrt(2.0 / math.pi) * (y + 0.044715 * (y * y * y))))
    ob = g * _sigmoid(_dot(g.astype(BF16), gw_ref[...]) + gb_ref[...])
    a0, a1 = A_WIDTH, A_WIDTH + B_WIDTH
    mixed = (_dot(oa_ref[...], w_ref[0:a0, :]) + _dot(ob.astype(BF16), w_ref[a0:a1, :])
             + _dot(oc_ref[...], w_ref[a1:, :]))
    o_ref[...] = x_ref[...] + (1.0 + gt_ref[...]) * _rms(mixed, gpost_ref[...])


def _outproj(x, oa, yc, u, s5_d, glu_w, glu_b, oc, w_out, mod, layer, norm_post, tm, s5_chunk):
    n, d = x.shape
    row = lambda w: pl.BlockSpec((tm, w), lambda i: (i, 0))
    lvec = lambda w: pl.BlockSpec((None, 1, w), lambda i: (layer, 0, 0))
    return pl.pallas_call(
        functools.partial(_outproj_kernel, s5_chunk=s5_chunk),
        scratch_shapes=[pltpu.VMEM((tm, V7X_LANES), F32), pltpu.VMEM((tm, V7X_LANES), F32)],
        grid=(n // tm,),
        in_specs=[
            row(d), row(A_WIDTH), pl.BlockSpec((tm // s5_chunk, s5_chunk * B_WIDTH), lambda i: (i, 0)),
            row(B_WIDTH), lvec(B_WIDTH),
            _resident((None, B_WIDTH, B_WIDTH), lambda i: (layer, 0, 0)), lvec(B_WIDTH),
            row(C_WIDTH), _resident((None, d, d), lambda i: (layer, 0, 0)),
            mod.spec(1, 2, tm), pl.BlockSpec((None, None, 1, d), lambda i: (layer, 1, 0, 0)),
        ],
        out_specs=row(d),
        out_shape=jax.ShapeDtypeStruct((n, d), F32),
        compiler_params=_cparams(("parallel",)),
        name="outproj",
    )(x, oa, yc, u, s5_d, glu_w, glu_b, oc, w_out, mod.arr, norm_post)


def kernel(x_prompt, x_sample, cache_k, cache_v, state_s5_re, state_s5_im, state_gla, page_table, c_prompt, c_sample, ada_w, ada_b, norm_pre, norm_post, ffn1_wi, ffn1_wo, ffn2_wi, ffn2_wo, w_in, w_out, lam_q1, lam_k1, lam_q2, lam_k2, subln_g, s5_lam_re, s5_lam_im, s5_b_re, s5_b_im, s5_c_re, s5_c_im, s5_d, s5_log_dt, s5_glu_w, s5_glu_b, gla_gate_w2, gla_gate_b, gla_norm_g):
    depth = ada_w.shape[0]
    batch, seq, d = x_prompt.shape
    dec_b, dec_t, _ = x_sample.shape
    n_pages = page_table.shape[1]
    page = cache_k.shape[2]
    past_len = n_pages * page
    n_p, n_s = batch * seq, dec_b * dec_t
    tm_p, tm_s = min(ROW_TILE, seq), min(ROW_TILE, n_s)
    tf_p = min(FFN_TILE, seq)

    n_c = batch + dec_b
    pad_c = (-n_c) % 8
    c_all = jnp.concatenate([c_prompt, c_sample, jnp.zeros((pad_c, d), F32)], axis=0)
    mod_all = _ada_mod(c_all, ada_w, ada_b)
    mod_p = mod_all[:, :batch].reshape(depth, batch, 1, N_SUB * 3 * d)
    mod_s = jnp.repeat(mod_all[:, batch:n_c], dec_t, axis=1)

    bf = lambda a: a.astype(BF16)
    wi1, wo1, wi2, wo2, w_out_b = bf(ffn1_wi), bf(ffn1_wo), bf(ffn2_wi), bf(ffn2_wo), bf(w_out)
    in_dim = w_in.shape[2]
    w_in_b = bf(jnp.pad(w_in, ((0, 0), (0, 0), (0, V7X_LANES - GLA_RANK))))
    assert w_in_b.shape[2] == in_dim - GLA_RANK + V7X_LANES
    w2p = bf(jnp.pad(gla_gate_w2, ((0, 0), (0, V7X_LANES - GLA_RANK), (0, 0))))
    b2 = gla_gate_b.reshape(depth, 1, GLA_QK)
    glu_w = bf(s5_glu_w)
    glu_b = s5_glu_b.reshape(depth, 1, B_WIDTH)
    s5_d3 = s5_d.reshape(depth, 1, B_WIDTH)
    npre = norm_pre.reshape(depth, N_SUB, 1, d)
    npost = norm_post.reshape(depth, N_SUB, 1, d)
    gn_vec = jnp.tile(gla_norm_g, (1, GLA_HEADS)).reshape(depth, 1, C_WIDTH)

    tabs_p = _rope_tables(jnp.arange(seq, dtype=jnp.int32))
    tabs_s = _rope_tables(past_len + jnp.arange(n_s, dtype=jnp.int32) % dec_t)

    cache_k4 = cache_k.reshape(depth, cache_k.shape[1], page * A_HEADS, A_HD)
    cache_v4 = cache_v.reshape(depth, cache_v.shape[1], page * A_HEADS, A_HD)

    def pad_tokens(a, edge=False):
        a3 = a.reshape(dec_b, dec_t, a.shape[-1])
        return jnp.pad(a3, ((0, 0), (0, DEC_PAD - dec_t), (0, 0)), mode='edge' if edge else 'constant')

    yp = x_prompt.reshape(n_p, d)
    ys = x_sample.reshape(n_s, d)
    outs_p, outs_s = [], []
    for l in range(depth):
        mp = _Mod(mod_p, l, seq, d)
        ms = _Mod(mod_s, l, n_s, d)
        lam_init = 0.8 - 0.6 * math.exp(-0.3 * l)
        lam = (jnp.exp(jnp.sum(lam_q1[l] * lam_k1[l])) - jnp.exp(jnp.sum(lam_q2[l] * lam_k2[l])) + lam_init)
        lam_vec = jnp.full((1, A_HD), lam, F32)
        g_eff = (subln_g[l] * (1.0 - lam_init)).reshape(1, A_HD)

        yp = _ffn(yp, mp, 0, l, npre, npost, wi1, wo1, tf_p)
        ys = _ffn(ys, ms, 0, l, npre, npost, wi1, wo1, tm_s)

        lp = min(S5_CHUNK_PROMPT, seq)
        (qat_p, qbt_p, k32_p, kbf_p, v32_p, vt_p, u_p, uc_p, qkg_p, vg_p, gg_p, bc_p) = _inproj(
            yp, mp, l, npre, w_in_b, w2p, b2, tabs_p, batch, tm_p, GLA_BLOCK_PROMPT, lp, True)
        (qa_s, qb_s, k32_s, v32_s, u_s, uc_s, qkg_s, vg_s, gg_s, bc_s) = _inproj(
            ys, ms, l, npre, w_in_b, w2p, b2, tabs_s, 1, tm_s, dec_t, dec_t, False)

        oa_p = _attn_prompt(qat_p, qbt_p, kbf_p, vt_p, lam_vec, g_eff, batch, seq)

        def pad_new_rows(a):
            a3 = a.reshape(dec_b, dec_t * A_HEADS, A_HD)
            return jnp.pad(a3, ((0, 0), (0, (DEC_PAD - dec_t) * A_HEADS), (0, 0)))

        oa_s8 = _attn_decode(pad_tokens(qa_s), pad_tokens(qb_s), pad_new_rows(k32_s), pad_new_rows(v32_s),
                             cache_k4, cache_v4, page_table, l, lam_vec, g_eff, dec_t)
        oa_s = oa_s8[:, :dec_t].reshape(n_s, A_WIDTH).astype(BF16)

        s5p = (s5_lam_re[l], s5_lam_im[l], s5_b_re[l], s5_b_im[l], s5_c_re[l], s5_c_im[l], s5_log_dt[l])
        bend, mtoep, ccar, a_chunk = _s5_matrices(*s5p, lp)
        e_p = _mm([(uc_p, bend)])
        xin_p, xlast_p = _s5_scan(e_p, a_chunk, batch)
        yc_p = _mm([(uc_p, mtoep), (xin_p, ccar)])
        bend4, mtoep4, ccar4, a_chunk4 = _s5_matrices(*s5p, dec_t)
        x0 = jnp.concatenate([state_s5_re[l].reshape(dec_b, S5_GP), state_s5_im[l].reshape(dec_b, S5_GP)], axis=1)
        xlast_s = _s5_step(_mm([(uc_s, bend4)]), a_chunk4, x0)
        yc_s = _mm([(uc_s, mtoep4), (x0, ccar4)])

        oc_p, gla_p = _gla(qkg_p, vg_p, bc_p, gg_p, gn_vec[l], None, batch, seq, GLA_BLOCK_PROMPT, GLA_TILE)
        flat8 = lambda a, edge=False: pad_tokens(a, edge).reshape(dec_b * DEC_PAD, a.shape[-1])
        oc_s8, gla_s = _gla(flat8(qkg_s), flat8(vg_s), flat8(bc_s, True), flat8(gg_s), gn_vec[l],
                            state_gla[l].reshape(dec_b, GLA_QK, GLA_DV), dec_b, DEC_PAD, DEC_PAD, DEC_PAD)
        oc_s = oc_s8.reshape(dec_b, DEC_PAD, C_WIDTH)[:, :dec_t].reshape(n_s, C_WIDTH)

        yp = _outproj(yp, oa_p, yc_p, u_p, s5_d3, glu_w, glu_b, oc_p, w_out_b, mp, l, npost, tm_p, lp)
        ys = _outproj(ys, oa_s, yc_s, u_s, s5_d3, glu_w, glu_b, oc_s, w_out_b, ms, l, npost, tm_s, dec_t)

        yp = _ffn(yp, mp, 2, l, npre, npost, wi2, wo2, tf_p)
        ys = _ffn(ys, ms, 2, l, npre, npost, wi2, wo2, tm_s)

        outs_p.append((k32_p.reshape(batch, seq, A_HEADS, A_HD), v32_p.reshape(batch, seq, A_HEADS, A_HD),
                       xlast_p[:, :S5_GP].reshape(batch, S5_GROUPS, S5_P),
                       xlast_p[:, S5_GP:].reshape(batch, S5_GROUPS, S5_P),
                       gla_p.reshape(batch, GLA_HEADS, GLA_DK, GLA_DV)))
        outs_s.append((k32_s.reshape(dec_b, dec_t, A_HEADS, A_HD), v32_s.reshape(dec_b, dec_t, A_HEADS, A_HD),
                       xlast_s[:, :S5_GP].reshape(dec_b, S5_GROUPS, S5_P),
                       xlast_s[:, S5_GP:].reshape(dec_b, S5_GROUPS, S5_P),
                       gla_s.reshape(dec_b, GLA_HEADS, GLA_DK, GLA_DV)))

    kp, vp, srp, sip, gp = [jnp.stack(z) for z in zip(*outs_p)]
    ks, vs, srs, sis, gs = [jnp.stack(z) for z in zip(*outs_s)]
    return (yp.reshape(batch, seq, d), ys.reshape(dec_b, dec_t, d), kp, vp, srp, sip, gp, ks, vs, srs, sis, gs)
```

```python
import functools
import math

import jax
import jax.numpy as jnp
import numpy as np
from jax import lax
from jax.experimental import pallas as pl
from jax.experimental.pallas import tpu as pltpu

F32 = jnp.float32
BF16 = jnp.bfloat16

A_HEADS = 4
A_DH = 64
A_HD = 2 * A_DH
A_WIDTH = A_HEADS * A_HD
ROT_DIM = 16
ROPE_THETA = 500000.0
NEG_INF = -1e30
LOG2_E = math.log2(math.e)
S5_GROUPS = 16
S5_GROUP = 16
S5_P = 64
B_WIDTH = S5_GROUPS * S5_GROUP
S5_GP = S5_GROUPS * S5_P
GLA_HEADS = 4
GLA_DK = 32
GLA_DV = 64
GLA_QK = GLA_HEADS * GLA_DK
C_WIDTH = GLA_HEADS * GLA_DV
GLA_RANK = 16
GLA_TAU = 16.0
MACARON_W = 0.5
RMS_EPS = 1e-6
N_SUB = 3

V7X_LANES = 128
V7X_VMEM_LIMIT_BYTES = 56 * 1024 * 1024

ROW_TILE = 512
FFN_TILE = 1024
FFN_CHUNK = 256
ATTN_TQ = 1024
ATTN_TK = 1024
S5_CHUNK_PROMPT = 8
GLA_BLOCK_PROMPT = 32
GLA_TILE = 512
GLA_CHAINS_PROMPT = 2
GLA_CHAINS_DECODE = 8
DEC_PAD = 8


def _cparams(sem):
    return pltpu.CompilerParams(dimension_semantics=sem, vmem_limit_bytes=V7X_VMEM_LIMIT_BYTES)


def _resident(shape, index_map):
    return pl.BlockSpec(shape, index_map, pipeline_mode=pl.Buffered(1))


def _rms(x, g):
    return x * lax.rsqrt(jnp.mean(x * x, axis=-1, keepdims=True) + RMS_EPS) * g


def _sigmoid(x):
    return 1.0 / (1.0 + jnp.exp(-x))


def _group_of(idx, size):
    assert size & (size - 1) == 0
    return jnp.right_shift(idx, size.bit_length() - 1)


def _dot(a, b):
    return jnp.dot(a, b, preferred_element_type=F32)


def _dot_nt(a, b):
    return lax.dot_general(a, b, (((1,), (1,)), ((), ())), preferred_element_type=F32)


def _dot_tn(a, b):
    return lax.dot_general(a, b, (((0,), (0,)), ((), ())), preferred_element_type=F32)


def _ada_kernel(c_ref, w_ref, b_ref, o_ref):
    c = c_ref[...]
    a = (c * _sigmoid(c)).astype(BF16)
    o_ref[...] = _dot(a, w_ref[...].astype(BF16)) + b_ref[...]


def _ada_mod(c_all, ada_w, ada_b):
    depth, d, n = ada_w.shape
    rows = c_all.shape[0]
    tn = n // 4
    return pl.pallas_call(
        _ada_kernel,
        grid=(depth, n // tn),
        in_specs=[
            pl.BlockSpec((rows, d), lambda l, j: (0, 0)),
            pl.BlockSpec((None, d, tn), lambda l, j: (l, 0, j)),
            pl.BlockSpec((None, 1, tn), lambda l, j: (l, 0, j)),
        ],
        out_specs=pl.BlockSpec((None, rows, tn), lambda l, j: (l, 0, j)),
        out_shape=jax.ShapeDtypeStruct((depth, rows, n), F32),
        compiler_params=_cparams(("arbitrary", "arbitrary")),
        name="ada_mod",
    )(c_all, ada_w, ada_b.reshape(depth, 1, n))


class _Mod:
    def __init__(self, arr, layer, rows_per_seq, d):
        self.arr = arr
        self.layer = layer
        self.d = d
        self.per_row = arr.ndim == 3
        self.rows_per_seq = rows_per_seq

    def spec(self, sub, kind, tm):
        col = N_SUB * sub + kind
        l = self.layer
        if self.per_row:
            return pl.BlockSpec((None, tm, self.d), lambda i: (l, i, col))
        tps = self.rows_per_seq // tm
        return pl.BlockSpec((None, None, 1, self.d), lambda i: (l, i // tps, 0, col))


def _ffn_kernel(x_ref, sh_ref, sc_ref, gt_ref, gpre_ref, gpost_ref, wi_ref, wo_ref, o_ref, *, d_ff, chunk):
    x = x_ref[...]
    h = (_rms(x, gpre_ref[...]) * (1.0 + sc_ref[...]) + sh_ref[...]).astype(BF16)
    acc = jnp.zeros(x.shape, F32)
    for c in range(d_ff // chunk):
        lo = c * chunk
        g = _dot(h, wi_ref[:, lo:lo + chunk])
        u = _dot(h, wi_ref[:, d_ff + lo:d_ff + lo + chunk])
        a = (g * _sigmoid(g) * u).astype(BF16)
        acc = acc + _dot(a, wo_ref[lo:lo + chunk, :])
    o_ref[...] = x + MACARON_W * (1.0 + gt_ref[...]) * _rms(acc, gpost_ref[...])


def _ffn(x, mod, sub, layer, norm_pre, norm_post, wi, wo, tm):
    n, d = x.shape
    d_ff = wo.shape[1]
    vec = pl.BlockSpec((None, None, 1, d), lambda i: (layer, sub, 0, 0))
    return pl.pallas_call(
        functools.partial(_ffn_kernel, d_ff=d_ff, chunk=FFN_CHUNK),
        grid=(n // tm,),
        in_specs=[
            pl.BlockSpec((tm, d), lambda i: (i, 0)),
            mod.spec(sub, 0, tm), mod.spec(sub, 1, tm), mod.spec(sub, 2, tm),
            vec, vec,
            _resident((None, d, 2 * d_ff), lambda i: (layer, 0, 0)),
            _resident((None, d_ff, d), lambda i: (layer, 0, 0)),
        ],
        out_specs=pl.BlockSpec((tm, d), lambda i: (i, 0)),
        out_shape=jax.ShapeDtypeStruct((n, d), F32),
        compiler_params=_cparams(("parallel",)),
        name="ffn",
    )(x, mod.arr, mod.arr, mod.arr, norm_pre, norm_post, wi, wo)


def _inproj_kernel(x_ref, sh_ref, sc_ref, gpre_ref, w_ref, w2_ref, b2_ref, tc_ref, ts1_ref, ts2_ref, *refs,
                   gla_block, s5_chunk, transposed):
    out_refs, (ua_ref, ub_ref) = refs[:-2], refs[-2:]
    if transposed:
        qa_ref, qb_ref, k32_ref, kbf_ref, v32_ref, vt_ref, u_ref, uc_ref, qkg_ref, vg_ref, gg_ref, bc_ref = out_refs
    else:
        qa_ref, qb_ref, k32_ref, v32_ref, u_ref, uc_ref, qkg_ref, vg_ref, gg_ref, bc_ref = out_refs
    x = x_ref[...]
    tm = x.shape[0]
    h = (_rms(x, gpre_ref[...]) * (1.0 + sc_ref[...]) + sh_ref[...]).astype(BF16)
    tc, ts1, ts2 = tc_ref[...], ts1_ref[...], ts2_ref[...]
    if transposed:
        first_comp = lax.broadcasted_iota(jnp.int32, (A_HD, tm), 0) < A_DH
    else:
        first_comp = lax.broadcasted_iota(jnp.int32, (tm, A_HD), 1) < A_DH

    def rope(z):
        up = pltpu.roll(z, A_HD - ROT_DIM // 2, axis=1)
        dn = pltpu.roll(z, ROT_DIM // 2, axis=1)
        return z * tc + up * ts1 + dn * ts2

    w = A_WIDTH
    for hd in range(A_HEADS):
        lo = hd * A_HD
        if hd % 2 == 0:
            q2 = _dot(h, w_ref[:, lo:lo + 2 * A_HD])
            k2 = _dot(h, w_ref[:, w + lo:w + lo + 2 * A_HD])
            v2 = _dot(h, w_ref[:, 2 * w + lo:2 * w + lo + 2 * A_HD])
        part = slice((hd % 2) * A_HD, (hd % 2 + 1) * A_HD)
        q = rope(q2[:, part]) * (A_DH ** -0.5 * LOG2_E)
        k = rope(k2[:, part])
        v = v2[:, part]
        k32_ref[pl.ds(hd, tm, stride=A_HEADS), :] = k
        v32_ref[pl.ds(hd, tm, stride=A_HEADS), :] = v
        if transposed:
            qt = q.T
            qa_ref[lo:lo + A_HD, :] = jnp.where(first_comp, qt, 0.0).astype(BF16)
            qb_ref[lo:lo + A_HD, :] = jnp.where(first_comp, 0.0, qt).astype(BF16)
            kbf_ref[:, lo:lo + A_HD] = k.astype(BF16)
            vt_ref[lo:lo + A_HD, :] = v.T.astype(BF16)
        else:
            qa_ref[:, lo:lo + A_HD] = jnp.where(first_comp, q, 0.0)
            qb_ref[:, lo:lo + A_HD] = jnp.where(first_comp, 0.0, q)
    off = 3 * w
    u = _dot(h, w_ref[:, off:off + B_WIDTH])
    u_ref[...] = u
    ua_ref[...] = u[:, :V7X_LANES]
    ub_ref[...] = u[:, V7X_LANES:]
    for s in range(s5_chunk):
        for half, part_ref in enumerate((ua_ref, ub_ref)):
            c0 = s * B_WIDTH + half * V7X_LANES
            uc_ref[:, c0:c0 + V7X_LANES] = part_ref[pl.ds(s, tm // s5_chunk, stride=s5_chunk), :]
    off += B_WIDTH
    qkg_ref[...] = _dot(h, w_ref[:, off:off + 2 * GLA_QK])
    off += 2 * GLA_QK
    vg_ref[...] = _dot(h, w_ref[:, off:off + C_WIDTH])
    off += C_WIDTH
    gg_ref[...] = _dot(h, w_ref[:, off:off + C_WIDTH])
    off += C_WIDTH
    rg = _dot(h, w_ref[:, off:off + V7X_LANES]).astype(BF16)
    gate = _dot(rg, w2_ref[...]) + b2_ref[...]
    log_a = (jnp.minimum(gate, 0.0) - jnp.log(1.0 + jnp.exp(-jnp.abs(gate)))) / GLA_TAU
    r = lax.broadcasted_iota(jnp.int32, (tm, tm), 0)
    c = lax.broadcasted_iota(jnp.int32, (tm, tm), 1)
    same_block = _group_of(r, gla_block) == _group_of(c, gla_block)
    tri = jnp.where(same_block, jnp.where(c <= r, 1.0, 0.0), 0.0).astype(BF16)
    hi = log_a.astype(BF16)
    lo_part = (log_a - hi.astype(F32)).astype(BF16)
    bc_ref[...] = _dot(tri, hi) + _dot(tri, lo_part)


def _inproj(x, mod, layer, norm_pre, w_in, w2p, b2, tabs, batch, tm, gla_block, s5_chunk, transposed):
    n, d = x.shape
    n_cols = w_in.shape[2]
    seq = n // batch
    tps = seq // tm if transposed else None
    tab_tiles = tabs[0].shape[0] // tm
    row = lambda width: pl.BlockSpec((tm, width), lambda i: (i, 0))
    tab = pl.BlockSpec((tm, V7X_LANES), lambda i: (i % tab_tiles, 0))
    f32_row = lambda width: (row(width), jax.ShapeDtypeStruct((n, width), F32))
    head_rows = (pl.BlockSpec((tm * A_HEADS, A_HD), lambda i: (i, 0)), jax.ShapeDtypeStruct((n * A_HEADS, A_HD), F32))
    chunk_rows = (pl.BlockSpec((tm // s5_chunk, s5_chunk * B_WIDTH), lambda i: (i, 0)),
                  jax.ShapeDtypeStruct((n // s5_chunk, s5_chunk * B_WIDTH), F32))
    if transposed:
        tr = (pl.BlockSpec((None, A_WIDTH, tm), lambda i: (i // tps, 0, i % tps)),
              jax.ShapeDtypeStruct((batch, A_WIDTH, seq), BF16))
        outs = [tr, tr, head_rows, (row(A_WIDTH), jax.ShapeDtypeStruct((n, A_WIDTH), BF16)), head_rows, tr]
    else:
        outs = [f32_row(A_WIDTH), f32_row(A_WIDTH), head_rows, head_rows]
    outs += [f32_row(B_WIDTH), chunk_rows, f32_row(2 * GLA_QK), f32_row(C_WIDTH), f32_row(C_WIDTH), f32_row(GLA_QK)]
    return pl.pallas_call(
        functools.partial(_inproj_kernel, gla_block=gla_block, s5_chunk=s5_chunk, transposed=transposed),
        scratch_shapes=[pltpu.VMEM((tm, V7X_LANES), F32), pltpu.VMEM((tm, V7X_LANES), F32)],
        grid=(n // tm,),
        in_specs=[
            row(d), mod.spec(1, 0, tm), mod.spec(1, 1, tm),
            pl.BlockSpec((None, None, 1, d), lambda i: (layer, 1, 0, 0)),
            _resident((None, d, n_cols), lambda i: (layer, 0, 0)),
            _resident((None, V7X_LANES, GLA_QK), lambda i: (layer, 0, 0)),
            pl.BlockSpec((None, 1, GLA_QK), lambda i: (layer, 0, 0)),
            tab, tab, tab,
        ],
        out_specs=[spec for spec, _ in outs],
        out_shape=[shape for _, shape in outs],
        compiler_params=_cparams(("parallel",)),
        name="inproj",
    )(x, mod.arr, mod.arr, norm_pre, w_in, w2p, b2, *tabs)


def _rope_tables(pos):
    half = ROT_DIM // 2
    inv = ROPE_THETA ** (-jnp.arange(half, dtype=F32) / half)
    ang = pos.astype(F32)[:, None] * inv[None, :]
    cos, sin = jnp.cos(ang), jnp.sin(ang)
    lane = np.arange(V7X_LANES) % A_DH
    idx = lane % half
    is_lo = jnp.asarray(lane < half)
    is_hi = jnp.asarray((lane >= half) & (lane < ROT_DIM))
    cos_l, sin_l = cos[:, idx], sin[:, idx]
    tc = jnp.where(is_lo | is_hi, cos_l, 1.0)
    ts1 = jnp.where(is_lo, -sin_l, 0.0)
    ts2 = jnp.where(is_hi, sin_l, 0.0)
    return tc, ts1, ts2


def _attn_kernel(qi_tab, ki_tab, qa_ref, qb_ref, k_ref, vt_ref, lam_ref, g_ref, o_ref, m_ref, l_ref, acc_ref,
                 *, tq, tk):
    p = pl.program_id(2)
    qi = qi_tab[p]
    ki = ki_tab[p]
    q0 = qi * tq
    k0 = ki * tk

    @pl.when(ki == 0)
    def _():
        m_ref[...] = jnp.full(m_ref.shape, NEG_INF, F32)
        l_ref[...] = jnp.zeros(l_ref.shape, F32)
        acc_ref[...] = jnp.zeros(acc_ref.shape, F32)

    def step(masked):
        k = k_ref[...]
        vt = vt_ref[...]
        for c, q_ref in enumerate((qa_ref, qb_ref)):
            s = _dot(k, q_ref[...])
            if masked:
                key = k0 + lax.broadcasted_iota(jnp.int32, s.shape, 0)
                qry = q0 + lax.broadcasted_iota(jnp.int32, s.shape, 1)
                s = jnp.where(key <= qry, s, NEG_INF)
            m_prev = m_ref[c]
            m_new = jnp.maximum(m_prev, jnp.max(s, axis=0, keepdims=True))
            alpha = jnp.exp2(m_prev - m_new)
            pr = jnp.exp2(s - m_new)
            l_ref[c] = alpha * l_ref[c] + jnp.sum(pr, axis=0, keepdims=True)
            acc_ref[c] = alpha * acc_ref[c] + _dot(vt, pr.astype(BF16))
            m_ref[c] = m_new

    fully_visible = k0 + (tk - 1) <= q0

    @pl.when(fully_visible)
    def _():
        step(False)

    @pl.when(jnp.logical_not(fully_visible))
    def _():
        step(True)

    @pl.when(k0 + tk >= q0 + tq)
    def _():
        ot = acc_ref[0] / l_ref[0] - lam_ref[:, 0:1] * (acc_ref[1] / l_ref[1])
        ms = jnp.mean(ot * ot, axis=0, keepdims=True)
        o_ref[...] = ((ot * lax.rsqrt(ms + RMS_EPS)).T * g_ref[...]).astype(BF16)


def _attn_prompt(qat, qbt, kbf, vt, lam_vec, g_eff, batch, seq):
    tq = min(ATTN_TQ, seq)
    tk = min(ATTN_TK, seq)
    assert tq % tk == 0
    pairs = [(i, j) for i in range(seq // tq) for j in range(((i + 1) * tq) // tk)]
    qi_tab = jnp.asarray([a for a, _ in pairs], jnp.int32)
    ki_tab = jnp.asarray([b for _, b in pairs], jnp.int32)
    qspec = pl.BlockSpec((None, A_HD, tq), lambda b, h, p, qt, kt: (b, h, qt[p]))
    kspec = pl.BlockSpec((None, tk, A_HD), lambda b, h, p, qt, kt: (b, kt[p], h))
    vspec = pl.BlockSpec((None, A_HD, tk), lambda b, h, p, qt, kt: (b, h, kt[p]))
    ospec = pl.BlockSpec((None, tq, A_HD), lambda b, h, p, qt, kt: (b, qt[p], h))
    vec = pl.BlockSpec((1, A_HD), lambda b, h, p, qt, kt: (0, 0))
    out = pl.pallas_call(
        functools.partial(_attn_kernel, tq=tq, tk=tk),
        grid_spec=pltpu.PrefetchScalarGridSpec(
            num_scalar_prefetch=2,
            grid=(batch, A_HEADS, len(pairs)),
            in_specs=[qspec, qspec, kspec, vspec, vec, vec],
            out_specs=ospec,
            scratch_shapes=[
                pltpu.VMEM((2, 1, tq), F32), pltpu.VMEM((2, 1, tq), F32), pltpu.VMEM((2, A_HD, tq), F32),
            ],
        ),
        out_shape=jax.ShapeDtypeStruct((batch, seq, A_WIDTH), BF16),
        compiler_params=_cparams(("parallel", "parallel", "arbitrary")),
        name="attn_prompt",
    )(qi_tab, ki_tab, qat, qbt, kbf.reshape(batch, seq, A_WIDTH), vt, lam_vec, g_eff)
    return out.reshape(batch * seq, A_WIDTH)


def _attn_decode_kernel(pt_ref, *refs, n_pages, n_new):
    qa_ref, qb_ref, kn_ref, vn_ref, lam_ref, g_ref = refs[:6]
    kp = refs[6:6 + n_pages]
    vp = refs[6 + n_pages:6 + 2 * n_pages]
    o_ref = refs[6 + 2 * n_pages]
    del pt_ref
    rows = DEC_PAD
    grp = 2 * rows
    nq = A_HEADS * grp
    page_rows = kp[0].shape[0]
    head_lanes = [slice(hd * A_HD, (hd + 1) * A_HD) for hd in range(A_HEADS)]
    q_all = jnp.concatenate([r[:, lanes] for lanes in head_lanes for r in (qa_ref, qb_ref)], axis=0).astype(BF16)
    s = jnp.concatenate([_dot_nt(q_all, kp[j][...].astype(BF16)) for j in range(n_pages)], axis=1)
    row_head = _group_of(lax.broadcasted_iota(jnp.int32, s.shape, 0), grp)
    col_head = jnp.bitwise_and(lax.broadcasted_iota(jnp.int32, s.shape, 1), A_HEADS - 1)
    s = jnp.where(row_head == col_head, s, NEG_INF)
    s_new = _dot_nt(q_all, kn_ref[...].astype(BF16))
    r_new = lax.broadcasted_iota(jnp.int32, s_new.shape, 0)
    c_new = lax.broadcasted_iota(jnp.int32, s_new.shape, 1)
    tok = jnp.bitwise_and(r_new, rows - 1)
    visible = _group_of(c_new, A_HEADS) <= jnp.minimum(tok, n_new - 1)
    same_head = _group_of(r_new, grp) == jnp.bitwise_and(c_new, A_HEADS - 1)
    s_new = jnp.where(same_head, jnp.where(visible, s_new, NEG_INF), NEG_INF)
    m = jnp.maximum(jnp.max(s, axis=1, keepdims=True), jnp.max(s_new, axis=1, keepdims=True))
    pr = jnp.exp2(s - m)
    pr_new = jnp.exp2(s_new - m)
    inv_l = 1.0 / (jnp.sum(pr, axis=1, keepdims=True) + jnp.sum(pr_new, axis=1, keepdims=True))
    pr = pr * inv_l
    pr_new = pr_new * inv_l
    lam = lam_ref[:, 0:1]

    def diff(p):
        return jnp.concatenate([p[hd * grp:hd * grp + rows] - lam * p[hd * grp + rows:(hd + 1) * grp]
                                for hd in range(A_HEADS)], axis=0).astype(BF16)

    wgt = diff(pr)
    wgt_new = diff(pr_new)
    o = _dot(wgt_new, vn_ref[...].astype(BF16))
    for j in range(n_pages):
        o = o + _dot(wgt[:, j * page_rows:(j + 1) * page_rows], vp[j][...].astype(BF16))
    g = g_ref[...]
    for hd, lanes in enumerate(head_lanes):
        o_ref[:, lanes] = _rms(o[hd * rows:(hd + 1) * rows], g)


def _attn_decode(qa8, qb8, kn8, vn8, cache_k, cache_v, page_table, layer, lam_vec, g_eff, n_new):
    nb = qa8.shape[0]
    n_pages = page_table.shape[1]
    page_rows = cache_k.shape[2]
    tok_spec = pl.BlockSpec((None, DEC_PAD, A_WIDTH), lambda b, pt: (b, 0, 0))
    new_spec = pl.BlockSpec((None, DEC_PAD * A_HEADS, A_HD), lambda b, pt: (b, 0, 0))
    vec = pl.BlockSpec((1, A_HD), lambda b, pt: (0, 0))

    def page_spec(j):
        return pl.BlockSpec((None, None, page_rows, A_HD), lambda b, pt: (layer, pt[b, j], 0, 0))

    k_pages = [page_spec(j) for j in range(n_pages)]
    v_pages = [page_spec(j) for j in range(n_pages)]
    return pl.pallas_call(
        functools.partial(_attn_decode_kernel, n_pages=n_pages, n_new=n_new),
        grid_spec=pltpu.PrefetchScalarGridSpec(
            num_scalar_prefetch=1,
            grid=(nb,),
            in_specs=[tok_spec, tok_spec, new_spec, new_spec, vec, vec] + k_pages + v_pages,
            out_specs=tok_spec,
        ),
        out_shape=jax.ShapeDtypeStruct((nb, DEC_PAD, A_WIDTH), F32),
        compiler_params=_cparams(("arbitrary",)),
        name="attn_decode",
    )(page_table, qa8, qb8, kn8, vn8, lam_vec, g_eff, *([cache_k] * n_pages), *([cache_v] * n_pages))


def _mm_kernel(*refs, n_terms):
    o_ref = refs[-1]
    acc = None
    for i in range(n_terms):
        t = _dot(refs[2 * i][...].astype(BF16), refs[2 * i + 1][...])
        acc = t if acc is None else acc + t
    o_ref[...] = acc


def _mm(terms, tm=256, tn=1024):
    m = terms[0][0].shape[0]
    n = terms[0][1].shape[1]
    tm = min(tm, m)
    tn = min(tn, n)
    in_specs, args = [], []
    for x, w in terms:
        k = x.shape[1]
        in_specs += [pl.BlockSpec((tm, k), lambda i, j: (i, 0)), pl.BlockSpec((k, tn), lambda i, j: (0, j))]
        args += [x, w]
    return pl.pallas_call(
        functools.partial(_mm_kernel, n_terms=len(terms)),
        grid=(m // tm, n // tn),
        in_specs=in_specs,
        out_specs=pl.BlockSpec((tm, tn), lambda i, j: (i, j)),
        out_shape=jax.ShapeDtypeStruct((m, n), F32),
        compiler_params=_cparams(("parallel", "parallel")),
        name="s5_mm",
    )(*args)


def _s5_scan_kernel(e_ref, a_ref, xin_ref, xlast_ref):
    nc = e_ref.shape[0]
    gp = S5_GP
    are = a_ref[:, :gp]
    aim = a_ref[:, gp:]

    def body(j, carry):
        xr, xi = carry
        xin_ref[pl.ds(j, 1), :gp] = xr
        xin_ref[pl.ds(j, 1), gp:] = xi
        er = e_ref[pl.ds(j, 1), :gp]
        ei = e_ref[pl.ds(j, 1), gp:]
        return are * xr - aim * xi + er, are * xi + aim * xr + ei

    zero = jnp.zeros((1, gp), F32)
    xr, xi = lax.fori_loop(0, nc, body, (zero, zero))
    xlast_ref[:, :gp] = xr
    xlast_ref[:, gp:] = xi


def _s5_scan(e, a_chunk, batch):
    nc = e.shape[0] // batch
    w = e.shape[1]
    e3 = e.reshape(batch, nc, w)
    xin, xlast = pl.pallas_call(
        _s5_scan_kernel,
        grid=(batch,),
        in_specs=[pl.BlockSpec((None, nc, w), lambda b: (b, 0, 0)), pl.BlockSpec((1, w), lambda b: (0, 0))],
        out_specs=[pl.BlockSpec((None, nc, w), lambda b: (b, 0, 0)), pl.BlockSpec((None, 1, w), lambda b: (b, 0, 0))],
        out_shape=[jax.ShapeDtypeStruct((batch, nc, w), F32), jax.ShapeDtypeStruct((batch, 1, w), F32)],
        compiler_params=_cparams(("parallel",)),
        name="s5_scan",
    )(e3, a_chunk)
    return xin.reshape(batch * nc, w), xlast.reshape(batch, w)


def _s5_step_kernel(e_ref, a_ref, x0_ref, o_ref):
    gp = S5_GP
    are, aim = a_ref[:, :gp], a_ref[:, gp:]
    xr, xi = x0_ref[:, :gp], x0_ref[:, gp:]
    o_ref[:, :gp] = are * xr - aim * xi + e_ref[:, :gp]
    o_ref[:, gp:] = are * xi + aim * xr + e_ref[:, gp:]


def _s5_step(e, a_chunk, x0):
    return pl.pallas_call(
        _s5_step_kernel,
        out_shape=jax.ShapeDtypeStruct(e.shape, F32),
        name="s5_step",
    )(e, a_chunk, x0)


def _s5_matrices(lam_re, lam_im, b_re, b_im, c_re, c_im, log_dt, chunk):
    hp = lax.Precision.HIGHEST
    g, p, cc, L = S5_GROUPS, S5_P, S5_GROUP, chunk
    dt = jnp.exp(log_dt)[:, None]
    zr, zi = lam_re * dt, lam_im * dt
    d = jnp.arange(L + 1, dtype=F32)[:, None, None]
    mag = jnp.exp(zr[None] * d)
    pw_r, pw_i = mag * jnp.cos(zi[None] * d), mag * jnp.sin(zi[None] * d)
    ar, ai = pw_r[1] - 1.0, pw_i[1]
    den = lam_re * lam_re + lam_im * lam_im
    qr, qi = (ar * lam_re + ai * lam_im) / den, (ai * lam_re - ar * lam_im) / den
    bb_r = qr[..., None] * b_re - qi[..., None] * b_im
    bb_i = qr[..., None] * b_im + qi[..., None] * b_re
    def spread(a):
        x = a.shape[-1]
        lane_group = jnp.arange(g * x, dtype=jnp.int32) // x
        own = lane_group[None, None, None, :] == jnp.arange(g, dtype=jnp.int32)[None, :, None, None]
        return jnp.where(own, jnp.tile(a, (1, 1, 1, g)), 0.0)

    rev_r, rev_i = pw_r[L - 1::-1], pw_i[L - 1::-1]
    e_r = rev_r[..., None] * bb_r[None] - rev_i[..., None] * bb_i[None]
    e_i = rev_r[..., None] * bb_i[None] + rev_i[..., None] * bb_r[None]

    def bend_of(e):
        return spread(jnp.transpose(e, (0, 1, 3, 2))).reshape(L * g * cc, g * p)

    bend = jnp.concatenate([bend_of(e_r), bend_of(e_i)], axis=1)
    cp_r = c_re[None] * pw_r[:L, :, None, :] - c_im[None] * pw_i[:L, :, None, :]
    cp_i = c_re[None] * pw_i[:L, :, None, :] + c_im[None] * pw_r[:L, :, None, :]
    kd = (jnp.einsum('dgcp,gpe->dgce', cp_r, bb_r, precision=hp)
          - jnp.einsum('dgcp,gpe->dgce', cp_i, bb_i, precision=hp))
    kbd = spread(jnp.transpose(kd, (0, 1, 3, 2)))
    zero_blk = jnp.zeros((1,) + kbd.shape[1:], F32)
    cols = [jnp.concatenate([kbd[t::-1]] + [zero_blk] * (L - 1 - t), axis=0).reshape(L * g * cc, g * cc)
            for t in range(L)]
    mtoep = jnp.concatenate(cols, axis=1)
    c1_r = c_re[None] * pw_r[1:, :, None, :] - c_im[None] * pw_i[1:, :, None, :]
    c1_i = c_re[None] * pw_i[1:, :, None, :] + c_im[None] * pw_r[1:, :, None, :]

    def ccar_of(c1):
        per_t = spread(jnp.transpose(c1, (0, 1, 3, 2)))
        return jnp.concatenate([per_t[t].reshape(g * p, g * cc) for t in range(L)], axis=1)

    ccar = jnp.concatenate([ccar_of(c1_r), ccar_of(-c1_i)], axis=0)
    a_chunk = jnp.concatenate([pw_r[L].reshape(1, g * p), pw_i[L].reshape(1, g * p)], axis=1)
    return bend.astype(BF16), mtoep.astype(BF16), ccar.astype(BF16), a_chunk


def _gla_kernel(*refs, lb, n_blocks, n_chains, has_s0):
    nc = n_chains
    qk_refs, v_refs, bc_refs, gg_refs = (refs[i * nc:(i + 1) * nc] for i in range(4))
    gn_ref = refs[4 * nc]
    pos = 4 * nc + 1
    s0_ref = refs[pos] if has_s0 else None
    pos += int(has_s0)
    o_refs = refs[pos:pos + nc]
    s_out_ref, st_ref, obuf_ref = refs[pos + nc:]
    t = pl.program_id(1)
    kw, vw = GLA_QK, C_WIDTH
    def same_head(shape, row_size, col_size):
        r = _group_of(lax.broadcasted_iota(jnp.int32, shape, 0), row_size)
        c = _group_of(lax.broadcasted_iota(jnp.int32, shape, 1), col_size)
        return r == c

    bd_mask = same_head((vw, kw), GLA_DV, GLA_DK)

    @pl.when(t == 0)
    def _():
        if has_s0:
            for c in range(nc):
                s0t = s0_ref[c].T
                st_ref[c] = jnp.where(bd_mask, jnp.concatenate([s0t] * GLA_HEADS, axis=0), 0.0)
        else:
            st_ref[...] = jnp.zeros(st_ref.shape, F32)

    head_expand = jnp.where(same_head((kw, vw), GLA_DK, GLA_DV), 1.0, 0.0).astype(BF16)
    row = lax.broadcasted_iota(jnp.int32, (lb, kw), 0)
    scale = GLA_DK ** -0.5

    def block(c, r0):
        q = qk_refs[c][pl.ds(r0, lb), 0:kw] * scale
        k = qk_refs[c][pl.ds(r0, lb), kw:2 * kw]
        v = v_refs[c][pl.ds(r0, lb), :]
        bc = bc_refs[c][pl.ds(r0, lb), :]
        bl = bc[lb - 1:lb, :]
        st = st_ref[c]
        o = _dot_nt((q * jnp.exp(bc)).astype(BF16), st.astype(BF16))
        parts = []
        for s in range(lb):
            dec = jnp.exp(jnp.minimum(bc - bc[s:s + 1, :], 0.0))
            parts.append(jnp.where(row >= s, q * k[s:s + 1, :] * dec, 0.0))
        att = _dot(jnp.concatenate(parts, axis=0).astype(BF16), head_expand)
        for s in range(lb):
            o = o + att[s * lb:(s + 1) * lb, :] * v[s:s + 1, :]
        obuf_ref[c, pl.ds(r0, lb), :] = o
        ke = (k * jnp.exp(bl - bc)).astype(BF16)
        upd = _dot_tn(v.astype(BF16), ke)
        st_ref[c] = jnp.exp(bl) * st + jnp.where(bd_mask, upd, 0.0)

    def body(i, carry):
        r0 = pl.multiple_of(i * lb, lb)
        for c in range(nc):
            block(c, r0)
        return carry

    lax.fori_loop(0, n_blocks, body, 0)

    seg_ones = jnp.where(same_head((vw, vw), GLA_DV, GLA_DV), 1.0, 0.0).astype(BF16)
    for c in range(nc):
        o = obuf_ref[c]
        ms = _dot((o * o).astype(BF16), seg_ones) * (1.0 / GLA_DV)
        gg = gg_refs[c][...]
        o_refs[c][...] = (o * lax.rsqrt(ms + RMS_EPS) * gn_ref[...] * (gg * _sigmoid(gg))).astype(BF16)

    @pl.when(t == pl.num_programs(1) - 1)
    def _():
        for c in range(nc):
            st = st_ref[c]
            comp = st[0:GLA_DV]
            for hd in range(1, GLA_HEADS):
                comp = comp + st[hd * GLA_DV:(hd + 1) * GLA_DV]
            s_out_ref[c] = comp.T


def _gla(qkg, vg, bc, gg, gn_vec, s0, batch, seq, lb, tile, n_chains):
    tile = min(tile, seq)
    nt = seq // tile
    nc = n_chains
    assert batch % nc == 0

    def row(w, c):
        return pl.BlockSpec((tile, w), lambda g, t: ((g * nc + c) * nt + t, 0))

    st_spec = pl.BlockSpec((nc, GLA_QK, GLA_DV), lambda g, t: (g, 0, 0))
    in_specs, args = [], []
    for arr, w in ((qkg, 2 * GLA_QK), (vg, C_WIDTH), (bc, GLA_QK), (gg, C_WIDTH)):
        in_specs += [row(w, c) for c in range(nc)]
        args += [arr] * nc
    in_specs.append(pl.BlockSpec((1, C_WIDTH), lambda g, t: (0, 0)))
    args.append(gn_vec)
    if s0 is not None:
        in_specs.append(st_spec)
        args.append(s0)
    res = pl.pallas_call(
        functools.partial(_gla_kernel, lb=lb, n_blocks=tile // lb, n_chains=nc, has_s0=s0 is not None),
        grid=(batch // nc, nt),
        in_specs=in_specs,
        out_specs=[pl.BlockSpec((tile, C_WIDTH), lambda g, t: (g * nt + t, 0))] * nc + [st_spec],
        out_shape=[jax.ShapeDtypeStruct((batch // nc * seq, C_WIDTH), BF16)] * nc
        + [jax.ShapeDtypeStruct((batch, GLA_QK, GLA_DV), F32)],
        scratch_shapes=[pltpu.VMEM((nc, C_WIDTH, GLA_QK), F32), pltpu.VMEM((nc, tile, C_WIDTH), F32)],
        compiler_params=_cparams(("parallel", "arbitrary")),
        name="gla",
    )(*args)
    return res[:nc], res[nc]


def _outproj_kernel(x_ref, oa_ref, yc_ref, u_ref, d_ref, gw_ref, gb_ref, oc_ref, w_ref, gt_ref, gpost_ref, o_ref,
                    ya_ref, yb_ref, *, s5_chunk):
    tm = x_ref.shape[0]
    for s in range(s5_chunk):
        for half, part_ref in enumerate((ya_ref, yb_ref)):
            c0 = s * B_WIDTH + half * V7X_LANES
            part_ref[pl.ds(s, tm // s5_chunk, stride=s5_chunk), :] = yc_ref[:, c0:c0 + V7X_LANES]
    y = jnp.concatenate([ya_ref[...], yb_ref[...]], axis=1) + d_ref[...] * u_ref[...]
    g = 0.5 * y * (1.0 + jnp.tanh(math.sqrt(2.0 / math.pi) * (y + 0.044715 * (y * y * y))))
    ob = g * _sigmoid(_dot(g.astype(BF16), gw_ref[...]) + gb_ref[...])
    a0, a1 = A_WIDTH, A_WIDTH + B_WIDTH
    mixed = (_dot(oa_ref[...], w_ref[0:a0, :]) + _dot(ob.astype(BF16), w_ref[a0:a1, :])
             + _dot(oc_ref[...], w_ref[a1:, :]))
    o_ref[...] = x_ref[...] + (1.0 + gt_ref[...]) * _rms(mixed, gpost_ref[...])


def _outproj(x, oa, yc, u, s5_d, glu_w, glu_b, oc, w_out, mod, layer, norm_post, tm, s5_chunk):
    n, d = x.shape
    row = lambda w: pl.BlockSpec((tm, w), lambda i: (i, 0))
    lvec = lambda w: pl.BlockSpec((None, 1, w), lambda i: (layer, 0, 0))
    return pl.pallas_call(
        functools.partial(_outproj_kernel, s5_chunk=s5_chunk),
        scratch_shapes=[pltpu.VMEM((tm, V7X_LANES), F32), pltpu.VMEM((tm, V7X_LANES), F32)],
        grid=(n // tm,),
        in_specs=[
            row(d), row(A_WIDTH), pl.BlockSpec((tm // s5_chunk, s5_chunk * B_WIDTH), lambda i: (i, 0)),
            row(B_WIDTH), lvec(B_WIDTH),
            _resident((None, B_WIDTH, B_WIDTH), lambda i: (layer, 0, 0)), lvec(B_WIDTH),
            row(C_WIDTH), _resident((None, d, d), lambda i: (layer, 0, 0)),
            mod.spec(1, 2, tm), pl.BlockSpec((None, None, 1, d), lambda i: (layer, 1, 0, 0)),
        ],
        out_specs=row(d),
        out_shape=jax.ShapeDtypeStruct((n, d), F32),
        compiler_params=_cparams(("parallel",)),
        name="outproj",
    )(x, oa, yc, u, s5_d, glu_w, glu_b, oc, w_out, mod.arr, norm_post)


def kernel(x_prompt, x_sample, cache_k, cache_v, state_s5_re, state_s5_im, state_gla, page_table, c_prompt, c_sample, ada_w, ada_b, norm_pre, norm_post, ffn1_wi, ffn1_wo, ffn2_wi, ffn2_wo, w_in, w_out, lam_q1, lam_k1, lam_q2, lam_k2, subln_g, s5_lam_re, s5_lam_im, s5_b_re, s5_b_im, s5_c_re, s5_c_im, s5_d, s5_log_dt, s5_glu_w, s5_glu_b, gla_gate_w2, gla_gate_b, gla_norm_g):
    depth = ada_w.shape[0]
    batch, seq, d = x_prompt.shape
    dec_b, dec_t, _ = x_sample.shape
    n_pages = page_table.shape[1]
    page = cache_k.shape[2]
    past_len = n_pages * page
    n_p, n_s = batch * seq, dec_b * dec_t
    tm_p, tm_s = min(ROW_TILE, seq), min(ROW_TILE, n_s)
    tf_p = min(FFN_TILE, seq)

    n_c = batch + dec_b
    pad_c = (-n_c) % 8
    c_all = jnp.concatenate([c_prompt, c_sample, jnp.zeros((pad_c, d), F32)], axis=0)
    mod_all = _ada_mod(c_all, ada_w, ada_b)
    mod_p = mod_all[:, :batch].reshape(depth, batch, 1, N_SUB * 3 * d)
    mod_s = jnp.repeat(mod_all[:, batch:n_c], dec_t, axis=1)

    bf = lambda a: a.astype(BF16)
    wi1, wo1, wi2, wo2, w_out_b = bf(ffn1_wi), bf(ffn1_wo), bf(ffn2_wi), bf(ffn2_wo), bf(w_out)
    in_dim = w_in.shape[2]
    w_in_b = bf(jnp.pad(w_in, ((0, 0), (0, 0), (0, V7X_LANES - GLA_RANK))))
    assert w_in_b.shape[2] == in_dim - GLA_RANK + V7X_LANES
    w2p = bf(jnp.pad(gla_gate_w2, ((0, 0), (0, V7X_LANES - GLA_RANK), (0, 0))))
    b2 = gla_gate_b.reshape(depth, 1, GLA_QK)
    glu_w = bf(s5_glu_w)
    glu_b = s5_glu_b.reshape(depth, 1, B_WIDTH)
    s5_d3 = s5_d.reshape(depth, 1, B_WIDTH)
    npre = norm_pre.reshape(depth, N_SUB, 1, d)
    npost = norm_post.reshape(depth, N_SUB, 1, d)
    gn_vec = jnp.tile(gla_norm_g, (1, GLA_HEADS)).reshape(depth, 1, C_WIDTH)

    tabs_p = _rope_tables(jnp.arange(seq, dtype=jnp.int32))
    tabs_s = _rope_tables(past_len + jnp.arange(n_s, dtype=jnp.int32) % dec_t)

    cache_k4 = cache_k.reshape(depth, cache_k.shape[1], page * A_HEADS, A_HD)
    cache_v4 = cache_v.reshape(depth, cache_v.shape[1], page * A_HEADS, A_HD)

    def pad_tokens(a, edge=False):
        a3 = a.reshape(dec_b, dec_t, a.shape[-1])
        return jnp.pad(a3, ((0, 0), (0, DEC_PAD - dec_t), (0, 0)), mode='edge' if edge else 'constant')

    yp = x_prompt.reshape(n_p, d)
    ys = x_sample.reshape(n_s, d)
    outs_p, outs_s = [], []
    for l in range(depth):
        mp = _Mod(mod_p, l, seq, d)
        ms = _Mod(mod_s, l, n_s, d)
        lam_init = 0.8 - 0.6 * math.exp(-0.3 * l)
        lam = (jnp.exp(jnp.sum(lam_q1[l] * lam_k1[l])) - jnp.exp(jnp.sum(lam_q2[l] * lam_k2[l])) + lam_init)
        lam_vec = jnp.full((1, A_HD), lam, F32)
        g_eff = (subln_g[l] * (1.0 - lam_init)).reshape(1, A_HD)

        yp = _ffn(yp, mp, 0, l, npre, npost, wi1, wo1, tf_p)
        ys = _ffn(ys, ms, 0, l, npre, npost, wi1, wo1, tm_s)

        lp = min(S5_CHUNK_PROMPT, seq)
        (qat_p, qbt_p, k32_p, kbf_p, v32_p, vt_p, u_p, uc_p, qkg_p, vg_p, gg_p, bc_p) = _inproj(
            yp, mp, l, npre, w_in_b, w2p, b2, tabs_p, batch, tm_p, GLA_BLOCK_PROMPT, lp, True)
        (qa_s, qb_s, k32_s, v32_s, u_s, uc_s, qkg_s, vg_s, gg_s, bc_s) = _inproj(
            ys, ms, l, npre, w_in_b, w2p, b2, tabs_s, 1, tm_s, dec_t, dec_t, False)

        oa_p = _attn_prompt(qat_p, qbt_p, kbf_p, vt_p, lam_vec, g_eff, batch, seq)

        def pad_new_rows(a):
            a3 = a.reshape(dec_b, dec_t * A_HEADS, A_HD)
            return jnp.pad(a3, ((0, 0), (0, (DEC_PAD - dec_t) * A_HEADS), (0, 0)))

        oa_s8 = _attn_decode(pad_tokens(qa_s), pad_tokens(qb_s), pad_new_rows(k32_s), pad_new_rows(v32_s),
                             cache_k4, cache_v4, page_table, l, lam_vec, g_eff, dec_t)
        oa_s = oa_s8[:, :dec_t].reshape(n_s, A_WIDTH).astype(BF16)

        s5p = (s5_lam_re[l], s5_lam_im[l], s5_b_re[l], s5_b_im[l], s5_c_re[l], s5_c_im[l], s5_log_dt[l])
        bend, mtoep, ccar, a_chunk = _s5_matrices(*s5p, lp)
        e_p = _mm([(uc_p, bend)])
        xin_p, xlast_p = _s5_scan(e_p, a_chunk, batch)
        yc_p = _mm([(uc_p, mtoep), (xin_p, ccar)])
        bend4, mtoep4, ccar4, a_chunk4 = _s5_matrices(*s5p, dec_t)
        x0 = jnp.concatenate([state_s5_re[l].reshape(dec_b, S5_GP), state_s5_im[l].reshape(dec_b, S5_GP)], axis=1)
        xlast_s = _s5_step(_mm([(uc_s, bend4)]), a_chunk4, x0)
        yc_s = _mm([(uc_s, mtoep4), (x0, ccar4)])

        ncp = math.gcd(batch, GLA_CHAINS_PROMPT)
        oc_parts, gla_p = _gla(qkg_p, vg_p, bc_p, gg_p, gn_vec[l], None, batch, seq, GLA_BLOCK_PROMPT, GLA_TILE, ncp)
        oc_p = jnp.stack([o.reshape(batch // ncp, seq, C_WIDTH) for o in oc_parts], axis=1).reshape(n_p, C_WIDTH)
        flat8 = lambda a, edge=False: pad_tokens(a, edge).reshape(dec_b * DEC_PAD, a.shape[-1])
        ncs = math.gcd(dec_b, GLA_CHAINS_DECODE)
        oc_parts, gla_s = _gla(flat8(qkg_s), flat8(vg_s), flat8(bc_s, True), flat8(gg_s), gn_vec[l],
                               state_gla[l].reshape(dec_b, GLA_QK, GLA_DV), dec_b, DEC_PAD, DEC_PAD, DEC_PAD, ncs)
        oc_s8 = jnp.stack([o.reshape(dec_b // ncs, DEC_PAD, C_WIDTH) for o in oc_parts], axis=1)
        oc_s = oc_s8.reshape(dec_b, DEC_PAD, C_WIDTH)[:, :dec_t].reshape(n_s, C_WIDTH)

        yp = _outproj(yp, oa_p, yc_p, u_p, s5_d3, glu_w, glu_b, oc_p, w_out_b, mp, l, npost, tm_p, lp)
        ys = _outproj(ys, oa_s, yc_s, u_s, s5_d3, glu_w, glu_b, oc_s, w_out_b, ms, l, npost, tm_s, dec_t)

        yp = _ffn(yp, mp, 2, l, npre, npost, wi2, wo2, tf_p)
        ys = _ffn(ys, ms, 2, l, npre, npost, wi2, wo2, tm_s)

        outs_p.append((k32_p.reshape(batch, seq, A_HEADS, A_HD), v32_p.reshape(batch, seq, A_HEADS, A_HD),
                       xlast_p[:, :S5_GP].reshape(batch, S5_GROUPS, S5_P),
                       xlast_p[:, S5_GP:].reshape(batch, S5_GROUPS, S5_P),
                       gla_p.reshape(batch, GLA_HEADS, GLA_DK, GLA_DV)))
        outs_s.append((k32_s.reshape(dec_b, dec_t, A_HEADS, A_HD), v32_s.reshape(dec_b, dec_t, A_HEADS, A_HD),
                       xlast_s[:, :S5_GP].reshape(dec_b, S5_GROUPS, S5_P),
                       xlast_s[:, S5_GP:].reshape(dec_b, S5_GROUPS, S5_P),
                       gla_s.reshape(dec_b, GLA_HEADS, GLA_DK, GLA_DV)))

    kp, vp, srp, sip, gp = [jnp.stack(z) for z in zip(*outs_p)]
    ks, vs, srs, sis, gs = [jnp.stack(z) for z in zip(*outs_s)]
    return (yp.reshape(batch, seq, d), ys.reshape(dec_b, dec_t, d), kp, vp, srp, sip, gp, ks, vs, srs, sis, gs)
```

```python
import functools
import math

import jax
import jax.numpy as jnp
import numpy as np
from jax import lax
from jax.experimental import pallas as pl
from jax.experimental.pallas import tpu as pltpu

F32 = jnp.float32
BF16 = jnp.bfloat16

A_HEADS = 4
A_DH = 64
A_HD = 2 * A_DH
A_WIDTH = A_HEADS * A_HD
ROT_DIM = 16
ROPE_THETA = 500000.0
NEG_INF = -1e30
LOG2_E = math.log2(math.e)
S5_GROUPS = 16
S5_GROUP = 16
S5_P = 64
B_WIDTH = S5_GROUPS * S5_GROUP
S5_GP = S5_GROUPS * S5_P
GLA_HEADS = 4
GLA_DK = 32
GLA_DV = 64
GLA_QK = GLA_HEADS * GLA_DK
C_WIDTH = GLA_HEADS * GLA_DV
GLA_RANK = 16
GLA_TAU = 16.0
MACARON_W = 0.5
RMS_EPS = 1e-6
N_SUB = 3

V7X_LANES = 128
V7X_VMEM_LIMIT_BYTES = 56 * 1024 * 1024

ROW_TILE = 512
FFN_TILE = 1024
FFN_CHUNK = 256
ATTN_TQ = 1024
ATTN_TK = 1024
S5_CHUNK_PROMPT = 8
GLA_BLOCK_PROMPT = 32
GLA_TILE = 512
GLA_CHAINS_PROMPT = 2
GLA_CHAINS_DECODE = 8
DEC_PAD = 8
DECODE_SEQS_PER_STEP = 2


def _cparams(sem):
    return pltpu.CompilerParams(dimension_semantics=sem, vmem_limit_bytes=V7X_VMEM_LIMIT_BYTES)


def _resident(shape, index_map):
    return pl.BlockSpec(shape, index_map, pipeline_mode=pl.Buffered(1))


def _rms(x, g):
    return x * lax.rsqrt(jnp.mean(x * x, axis=-1, keepdims=True) + RMS_EPS) * g


def _sigmoid(x):
    return 1.0 / (1.0 + jnp.exp(-x))


def _group_of(idx, size):
    assert size & (size - 1) == 0
    return jnp.right_shift(idx, size.bit_length() - 1)


def _dot(a, b):
    return jnp.dot(a, b, preferred_element_type=F32)


def _dot_nt(a, b):
    return lax.dot_general(a, b, (((1,), (1,)), ((), ())), preferred_element_type=F32)


def _dot_tn(a, b):
    return lax.dot_general(a, b, (((0,), (0,)), ((), ())), preferred_element_type=F32)


def _ada_kernel(c_ref, w_ref, b_ref, o_ref):
    c = c_ref[...]
    a = (c * _sigmoid(c)).astype(BF16)
    o_ref[...] = _dot(a, w_ref[...].astype(BF16)) + b_ref[...]


def _ada_mod(c_all, ada_w, ada_b):
    depth, d, n = ada_w.shape
    rows = c_all.shape[0]
    tn = n // 4
    return pl.pallas_call(
        _ada_kernel,
        grid=(depth, n // tn),
        in_specs=[
            pl.BlockSpec((rows, d), lambda l, j: (0, 0)),
            pl.BlockSpec((None, d, tn), lambda l, j: (l, 0, j)),
            pl.BlockSpec((None, 1, tn), lambda l, j: (l, 0, j)),
        ],
        out_specs=pl.BlockSpec((None, rows, tn), lambda l, j: (l, 0, j)),
        out_shape=jax.ShapeDtypeStruct((depth, rows, n), F32),
        compiler_params=_cparams(("arbitrary", "arbitrary")),
        name="ada_mod",
    )(c_all, ada_w, ada_b.reshape(depth, 1, n))


class _Mod:
    def __init__(self, arr, layer, rows_per_seq, d):
        self.arr = arr
        self.layer = layer
        self.d = d
        self.per_row = arr.ndim == 3
        self.rows_per_seq = rows_per_seq

    def spec(self, sub, kind, tm):
        col = N_SUB * sub + kind
        l = self.layer
        if self.per_row:
            return pl.BlockSpec((None, tm, self.d), lambda i: (l, i, col))
        tps = self.rows_per_seq // tm
        return pl.BlockSpec((None, None, 1, self.d), lambda i: (l, i // tps, 0, col))


def _ffn_kernel(x_ref, sh_ref, sc_ref, gt_ref, gpre_ref, gpost_ref, wi_ref, wo_ref, o_ref, *, d_ff, chunk):
    x = x_ref[...]
    h = (_rms(x, gpre_ref[...]) * (1.0 + sc_ref[...]) + sh_ref[...]).astype(BF16)
    acc = jnp.zeros(x.shape, F32)
    for c in range(d_ff // chunk):
        lo = c * chunk
        g = _dot(h, wi_ref[:, lo:lo + chunk])
        u = _dot(h, wi_ref[:, d_ff + lo:d_ff + lo + chunk])
        a = (g * _sigmoid(g) * u).astype(BF16)
        acc = acc + _dot(a, wo_ref[lo:lo + chunk, :])
    o_ref[...] = x + MACARON_W * (1.0 + gt_ref[...]) * _rms(acc, gpost_ref[...])


def _ffn(x, mod, sub, layer, norm_pre, norm_post, wi, wo, tm):
    n, d = x.shape
    d_ff = wo.shape[1]
    vec = pl.BlockSpec((None, None, 1, d), lambda i: (layer, sub, 0, 0))
    return pl.pallas_call(
        functools.partial(_ffn_kernel, d_ff=d_ff, chunk=FFN_CHUNK),
        grid=(n // tm,),
        in_specs=[
            pl.BlockSpec((tm, d), lambda i: (i, 0)),
            mod.spec(sub, 0, tm), mod.spec(sub, 1, tm), mod.spec(sub, 2, tm),
            vec, vec,
            _resident((None, d, 2 * d_ff), lambda i: (layer, 0, 0)),
            _resident((None, d_ff, d), lambda i: (layer, 0, 0)),
        ],
        out_specs=pl.BlockSpec((tm, d), lambda i: (i, 0)),
        out_shape=jax.ShapeDtypeStruct((n, d), F32),
        compiler_params=_cparams(("parallel",)),
        name="ffn",
    )(x, mod.arr, mod.arr, mod.arr, norm_pre, norm_post, wi, wo)


def _inproj_kernel(x_ref, sh_ref, sc_ref, gpre_ref, w_ref, w2_ref, b2_ref, tc_ref, ts1_ref, ts2_ref, *refs,
                   gla_block, s5_chunk, transposed):
    out_refs, (ua_ref, ub_ref) = refs[:-2], refs[-2:]
    if transposed:
        qa_ref, qb_ref, k32_ref, kbf_ref, v32_ref, vt_ref, u_ref, uc_ref, qkg_ref, vg_ref, gg_ref, bc_ref = out_refs
    else:
        qa_ref, qb_ref, k32_ref, v32_ref, u_ref, uc_ref, qkg_ref, vg_ref, gg_ref, bc_ref = out_refs
    x = x_ref[...]
    tm = x.shape[0]
    h = (_rms(x, gpre_ref[...]) * (1.0 + sc_ref[...]) + sh_ref[...]).astype(BF16)
    tc, ts1, ts2 = tc_ref[...], ts1_ref[...], ts2_ref[...]
    if transposed:
        first_comp = lax.broadcasted_iota(jnp.int32, (A_HD, tm), 0) < A_DH
    else:
        first_comp = lax.broadcasted_iota(jnp.int32, (tm, A_HD), 1) < A_DH

    def rope(z):
        up = pltpu.roll(z, A_HD - ROT_DIM // 2, axis=1)
        dn = pltpu.roll(z, ROT_DIM // 2, axis=1)
        return z * tc + up * ts1 + dn * ts2

    w = A_WIDTH
    for hd in range(A_HEADS):
        lo = hd * A_HD
        if hd % 2 == 0:
            q2 = _dot(h, w_ref[:, lo:lo + 2 * A_HD])
            k2 = _dot(h, w_ref[:, w + lo:w + lo + 2 * A_HD])
            v2 = _dot(h, w_ref[:, 2 * w + lo:2 * w + lo + 2 * A_HD])
        part = slice((hd % 2) * A_HD, (hd % 2 + 1) * A_HD)
        q = rope(q2[:, part]) * (A_DH ** -0.5 * LOG2_E)
        k = rope(k2[:, part])
        v = v2[:, part]
        k32_ref[pl.ds(hd, tm, stride=A_HEADS), :] = k
        v32_ref[pl.ds(hd, tm, stride=A_HEADS), :] = v
        if transposed:
            qt = q.T
            qa_ref[lo:lo + A_HD, :] = jnp.where(first_comp, qt, 0.0).astype(BF16)
            qb_ref[lo:lo + A_HD, :] = jnp.where(first_comp, 0.0, qt).astype(BF16)
            kbf_ref[:, lo:lo + A_HD] = k.astype(BF16)
            vt_ref[lo:lo + A_HD, :] = v.T.astype(BF16)
        else:
            qa_ref[:, lo:lo + A_HD] = jnp.where(first_comp, q, 0.0)
            qb_ref[:, lo:lo + A_HD] = jnp.where(first_comp, 0.0, q)
    off = 3 * w
    u = _dot(h, w_ref[:, off:off + B_WIDTH])
    u_ref[...] = u
    ua_ref[...] = u[:, :V7X_LANES]
    ub_ref[...] = u[:, V7X_LANES:]
    for s in range(s5_chunk):
        for half, part_ref in enumerate((ua_ref, ub_ref)):
            c0 = s * B_WIDTH + half * V7X_LANES
            uc_ref[:, c0:c0 + V7X_LANES] = part_ref[pl.ds(s, tm // s5_chunk, stride=s5_chunk), :]
    off += B_WIDTH
    qkg_ref[...] = _dot(h, w_ref[:, off:off + 2 * GLA_QK])
    off += 2 * GLA_QK
    vg_ref[...] = _dot(h, w_ref[:, off:off + C_WIDTH])
    off += C_WIDTH
    gg_ref[...] = _dot(h, w_ref[:, off:off + C_WIDTH])
    off += C_WIDTH
    rg = _dot(h, w_ref[:, off:off + V7X_LANES]).astype(BF16)
    gate = _dot(rg, w2_ref[...]) + b2_ref[...]
    log_a = (jnp.minimum(gate, 0.0) - jnp.log(1.0 + jnp.exp(-jnp.abs(gate)))) / GLA_TAU
    r = lax.broadcasted_iota(jnp.int32, (tm, tm), 0)
    c = lax.broadcasted_iota(jnp.int32, (tm, tm), 1)
    same_block = _group_of(r, gla_block) == _group_of(c, gla_block)
    tri = jnp.where(same_block, jnp.where(c <= r, 1.0, 0.0), 0.0).astype(BF16)
    hi = log_a.astype(BF16)
    lo_part = (log_a - hi.astype(F32)).astype(BF16)
    bc_ref[...] = _dot(tri, hi) + _dot(tri, lo_part)


def _inproj(x, mod, layer, norm_pre, w_in, w2p, b2, tabs, batch, tm, gla_block, s5_chunk, transposed):
    n, d = x.shape
    n_cols = w_in.shape[2]
    seq = n // batch
    tps = seq // tm if transposed else None
    tab_tiles = tabs[0].shape[0] // tm
    row = lambda width: pl.BlockSpec((tm, width), lambda i: (i, 0))
    tab = pl.BlockSpec((tm, V7X_LANES), lambda i: (i % tab_tiles, 0))
    f32_row = lambda width: (row(width), jax.ShapeDtypeStruct((n, width), F32))
    head_rows = (pl.BlockSpec((tm * A_HEADS, A_HD), lambda i: (i, 0)), jax.ShapeDtypeStruct((n * A_HEADS, A_HD), F32))
    chunk_rows = (pl.BlockSpec((tm // s5_chunk, s5_chunk * B_WIDTH), lambda i: (i, 0)),
                  jax.ShapeDtypeStruct((n // s5_chunk, s5_chunk * B_WIDTH), F32))
    if transposed:
        tr = (pl.BlockSpec((None, A_WIDTH, tm), lambda i: (i // tps, 0, i % tps)),
              jax.ShapeDtypeStruct((batch, A_WIDTH, seq), BF16))
        outs = [tr, tr, head_rows, (row(A_WIDTH), jax.ShapeDtypeStruct((n, A_WIDTH), BF16)), head_rows, tr]
    else:
        outs = [f32_row(A_WIDTH), f32_row(A_WIDTH), head_rows, head_rows]
    outs += [f32_row(B_WIDTH), chunk_rows, f32_row(2 * GLA_QK), f32_row(C_WIDTH), f32_row(C_WIDTH), f32_row(GLA_QK)]
    return pl.pallas_call(
        functools.partial(_inproj_kernel, gla_block=gla_block, s5_chunk=s5_chunk, transposed=transposed),
        scratch_shapes=[pltpu.VMEM((tm, V7X_LANES), F32), pltpu.VMEM((tm, V7X_LANES), F32)],
        grid=(n // tm,),
        in_specs=[
            row(d), mod.spec(1, 0, tm), mod.spec(1, 1, tm),
            pl.BlockSpec((None, None, 1, d), lambda i: (layer, 1, 0, 0)),
            _resident((None, d, n_cols), lambda i: (layer, 0, 0)),
            _resident((None, V7X_LANES, GLA_QK), lambda i: (layer, 0, 0)),
            pl.BlockSpec((None, 1, GLA_QK), lambda i: (layer, 0, 0)),
            tab, tab, tab,
        ],
        out_specs=[spec for spec, _ in outs],
        out_shape=[shape for _, shape in outs],
        compiler_params=_cparams(("parallel",)),
        name="inproj",
    )(x, mod.arr, mod.arr, norm_pre, w_in, w2p, b2, *tabs)


def _rope_tables(pos):
    half = ROT_DIM // 2
    inv = ROPE_THETA ** (-jnp.arange(half, dtype=F32) / half)
    ang = pos.astype(F32)[:, None] * inv[None, :]
    cos, sin = jnp.cos(ang), jnp.sin(ang)
    lane = np.arange(V7X_LANES) % A_DH
    idx = lane % half
    is_lo = jnp.asarray(lane < half)
    is_hi = jnp.asarray((lane >= half) & (lane < ROT_DIM))
    cos_l, sin_l = cos[:, idx], sin[:, idx]
    tc = jnp.where(is_lo | is_hi, cos_l, 1.0)
    ts1 = jnp.where(is_lo, -sin_l, 0.0)
    ts2 = jnp.where(is_hi, sin_l, 0.0)
    return tc, ts1, ts2


def _attn_kernel(qi_tab, ki_tab, qa_ref, qb_ref, k_ref, vt_ref, lam_ref, g_ref, o_ref, m_ref, l_ref, acc_ref,
                 *, tq, tk):
    p = pl.program_id(2)
    qi = qi_tab[p]
    ki = ki_tab[p]
    q0 = qi * tq
    k0 = ki * tk

    @pl.when(ki == 0)
    def _():
        m_ref[...] = jnp.full(m_ref.shape, NEG_INF, F32)
        l_ref[...] = jnp.zeros(l_ref.shape, F32)
        acc_ref[...] = jnp.zeros(acc_ref.shape, F32)

    def update(c, q_ref, rows, cols, masked):
        s = _dot(k_ref[rows, :], q_ref[:, cols])
        if masked:
            key = k0 + rows.start + lax.broadcasted_iota(jnp.int32, s.shape, 0)
            qry = q0 + cols.start + lax.broadcasted_iota(jnp.int32, s.shape, 1)
            s = jnp.where(key <= qry, s, NEG_INF)
        m_prev = m_ref[c, :, cols]
        m_new = jnp.maximum(m_prev, jnp.max(s, axis=0, keepdims=True))
        alpha = jnp.exp2(m_prev - m_new)
        pr = jnp.exp2(s - m_new)
        l_ref[c, :, cols] = alpha * l_ref[c, :, cols] + jnp.sum(pr, axis=0, keepdims=True)
        acc_ref[c, :, cols] = alpha * acc_ref[c, :, cols] + _dot(vt_ref[:, rows], pr.astype(BF16))
        m_ref[c, :, cols] = m_new

    def step(masked):
        for c, q_ref in enumerate((qa_ref, qb_ref)):
            if masked and tq == tk:
                half = tq // 2
                update(c, q_ref, slice(0, half), slice(0, half), True)
                update(c, q_ref, slice(0, tk), slice(half, tq), True)
            else:
                update(c, q_ref, slice(0, tk), slice(0, tq), masked)

    fully_visible = k0 + (tk - 1) <= q0

    @pl.when(fully_visible)
    def _():
        step(False)

    @pl.when(jnp.logical_not(fully_visible))
    def _():
        step(True)

    @pl.when(k0 + tk >= q0 + tq)
    def _():
        ot = acc_ref[0] / l_ref[0] - lam_ref[:, 0:1] * (acc_ref[1] / l_ref[1])
        ms = jnp.mean(ot * ot, axis=0, keepdims=True)
        o_ref[...] = ((ot * lax.rsqrt(ms + RMS_EPS)).T * g_ref[...]).astype(BF16)


def _attn_prompt(qat, qbt, kbf, vt, lam_vec, g_eff, batch, seq):
    tq = min(ATTN_TQ, seq)
    tk = min(ATTN_TK, seq)
    assert tq % tk == 0
    pairs = [(i, j) for i in range(seq // tq) for j in range(((i + 1) * tq) // tk)]
    qi_tab = jnp.asarray([a for a, _ in pairs], jnp.int32)
    ki_tab = jnp.asarray([b for _, b in pairs], jnp.int32)
    qspec = pl.BlockSpec((None, A_HD, tq), lambda b, h, p, qt, kt: (b, h, qt[p]))
    kspec = pl.BlockSpec((None, tk, A_HD), lambda b, h, p, qt, kt: (b, kt[p], h))
    vspec = pl.BlockSpec((None, A_HD, tk), lambda b, h, p, qt, kt: (b, h, kt[p]))
    ospec = pl.BlockSpec((None, tq, A_HD), lambda b, h, p, qt, kt: (b, qt[p], h))
    vec = pl.BlockSpec((1, A_HD), lambda b, h, p, qt, kt: (0, 0))
    out = pl.pallas_call(
        functools.partial(_attn_kernel, tq=tq, tk=tk),
        grid_spec=pltpu.PrefetchScalarGridSpec(
            num_scalar_prefetch=2,
            grid=(batch, A_HEADS, len(pairs)),
            in_specs=[qspec, qspec, kspec, vspec, vec, vec],
            out_specs=ospec,
            scratch_shapes=[
                pltpu.VMEM((2, 1, tq), F32), pltpu.VMEM((2, 1, tq), F32), pltpu.VMEM((2, A_HD, tq), F32),
            ],
        ),
        out_shape=jax.ShapeDtypeStruct((batch, seq, A_WIDTH), BF16),
        compiler_params=_cparams(("parallel", "parallel", "arbitrary")),
        name="attn_prompt",
    )(qi_tab, ki_tab, qat, qbt, kbf.reshape(batch, seq, A_WIDTH), vt, lam_vec, g_eff)
    return out.reshape(batch * seq, A_WIDTH)


def _attn_decode_kernel(pt_ref, *refs, n_pages, n_new, n_seq):
    del pt_ref
    qa_all, qb_all, kn_all, vn_all, lam_ref, g_ref = refs[:6]
    o_all = refs[6 + 2 * n_pages * n_seq]
    for i in range(n_seq):
        base = 6 + 2 * n_pages * i
        _attn_decode_one(qa_all.at[i], qb_all.at[i], kn_all.at[i], vn_all.at[i], lam_ref, g_ref,
                         refs[base:base + n_pages], refs[base + n_pages:base + 2 * n_pages], o_all.at[i], n_new)


def _attn_decode_one(qa_ref, qb_ref, kn_ref, vn_ref, lam_ref, g_ref, kp, vp, o_ref, n_new):
    n_pages = len(kp)
    rows = DEC_PAD
    grp = 2 * rows
    nq = A_HEADS * grp
    page_rows = kp[0].shape[0]
    head_lanes = [slice(hd * A_HD, (hd + 1) * A_HD) for hd in range(A_HEADS)]
    q_all = jnp.concatenate([r[:, lanes] for lanes in head_lanes for r in (qa_ref, qb_ref)], axis=0).astype(BF16)
    s = jnp.concatenate([_dot_nt(q_all, kp[j][...].astype(BF16)) for j in range(n_pages)], axis=1)
    row_head = _group_of(lax.broadcasted_iota(jnp.int32, s.shape, 0), grp)
    col_head = jnp.bitwise_and(lax.broadcasted_iota(jnp.int32, s.shape, 1), A_HEADS - 1)
    s = jnp.where(row_head == col_head, s, NEG_INF)
    s_new = _dot_nt(q_all, kn_ref[...].astype(BF16))
    r_new = lax.broadcasted_iota(jnp.int32, s_new.shape, 0)
    c_new = lax.broadcasted_iota(jnp.int32, s_new.shape, 1)
    tok = jnp.bitwise_and(r_new, rows - 1)
    visible = _group_of(c_new, A_HEADS) <= jnp.minimum(tok, n_new - 1)
    same_head = _group_of(r_new, grp) == jnp.bitwise_and(c_new, A_HEADS - 1)
    s_new = jnp.where(same_head, jnp.where(visible, s_new, NEG_INF), NEG_INF)
    m = jnp.maximum(jnp.max(s, axis=1, keepdims=True), jnp.max(s_new, axis=1, keepdims=True))
    pr = jnp.exp2(s - m)
    pr_new = jnp.exp2(s_new - m)
    inv_l = 1.0 / (jnp.sum(pr, axis=1, keepdims=True) + jnp.sum(pr_new, axis=1, keepdims=True))
    pr = pr * inv_l
    pr_new = pr_new * inv_l
    lam = lam_ref[:, 0:1]

    def diff(p):
        return jnp.concatenate([p[hd * grp:hd * grp + rows] - lam * p[hd * grp + rows:(hd + 1) * grp]
                                for hd in range(A_HEADS)], axis=0).astype(BF16)

    wgt = diff(pr)
    wgt_new = diff(pr_new)
    o = _dot(wgt_new, vn_ref[...].astype(BF16))
    for j in range(n_pages):
        o = o + _dot(wgt[:, j * page_rows:(j + 1) * page_rows], vp[j][...].astype(BF16))
    g = g_ref[...]
    for hd, lanes in enumerate(head_lanes):
        o_ref[:, lanes] = _rms(o[hd * rows:(hd + 1) * rows], g)


def _attn_decode(qa8, qb8, kn8, vn8, cache_k, cache_v, page_table, layer, lam_vec, g_eff, n_new):
    nb = qa8.shape[0]
    n_pages = page_table.shape[1]
    page_rows = cache_k.shape[2]
    n_seq = math.gcd(nb, DECODE_SEQS_PER_STEP)
    tok_spec = pl.BlockSpec((n_seq, DEC_PAD, A_WIDTH), lambda b, pt: (b, 0, 0))
    new_spec = pl.BlockSpec((n_seq, DEC_PAD * A_HEADS, A_HD), lambda b, pt: (b, 0, 0))
    vec = pl.BlockSpec((1, A_HD), lambda b, pt: (0, 0))

    def page_spec(i, j):
        return pl.BlockSpec((None, None, page_rows, A_HD), lambda b, pt: (layer, pt[b * n_seq + i, j], 0, 0))

    page_specs, page_args = [], []
    for i in range(n_seq):
        for cache in (cache_k, cache_v):
            page_specs += [page_spec(i, j) for j in range(n_pages)]
            page_args += [cache] * n_pages
    return pl.pallas_call(
        functools.partial(_attn_decode_kernel, n_pages=n_pages, n_new=n_new, n_seq=n_seq),
        grid_spec=pltpu.PrefetchScalarGridSpec(
            num_scalar_prefetch=1,
            grid=(nb // n_seq,),
            in_specs=[tok_spec, tok_spec, new_spec, new_spec, vec, vec] + page_specs,
            out_specs=tok_spec,
        ),
        out_shape=jax.ShapeDtypeStruct((nb, DEC_PAD, A_WIDTH), F32),
        compiler_params=_cparams(("arbitrary",)),
        name="attn_decode",
    )(page_table, qa8, qb8, kn8, vn8, lam_vec, g_eff, *page_args)


def _mm_kernel(*refs, n_terms):
    o_ref = refs[-1]
    acc = None
    for i in range(n_terms):
        t = _dot(refs[2 * i][...].astype(BF16), refs[2 * i + 1][...])
        acc = t if acc is None else acc + t
    o_ref[...] = acc


def _mm(terms, tm=256, tn=1024):
    m = terms[0][0].shape[0]
    n = terms[0][1].shape[1]
    tm = min(tm, m)
    tn = min(tn, n)
    in_specs, args = [], []
    for x, w in terms:
        k = x.shape[1]
        in_specs += [pl.BlockSpec((tm, k), lambda i, j: (i, 0)), pl.BlockSpec((k, tn), lambda i, j: (0, j))]
        args += [x, w]
    return pl.pallas_call(
        functools.partial(_mm_kernel, n_terms=len(terms)),
        grid=(m // tm, n // tn),
        in_specs=in_specs,
        out_specs=pl.BlockSpec((tm, tn), lambda i, j: (i, j)),
        out_shape=jax.ShapeDtypeStruct((m, n), F32),
        compiler_params=_cparams(("parallel", "parallel")),
        name="s5_mm",
    )(*args)


def _s5_scan_kernel(e_ref, a_ref, xin_ref, xlast_ref):
    nc = e_ref.shape[0]
    gp = S5_GP
    are = a_ref[:, :gp]
    aim = a_ref[:, gp:]

    def body(j, carry):
        xr, xi = carry
        xin_ref[pl.ds(j, 1), :gp] = xr
        xin_ref[pl.ds(j, 1), gp:] = xi
        er = e_ref[pl.ds(j, 1), :gp]
        ei = e_ref[pl.ds(j, 1), gp:]
        return are * xr - aim * xi + er, are * xi + aim * xr + ei

    zero = jnp.zeros((1, gp), F32)
    xr, xi = lax.fori_loop(0, nc, body, (zero, zero))
    xlast_ref[:, :gp] = xr
    xlast_ref[:, gp:] = xi


def _s5_scan(e, a_chunk, batch):
    nc = e.shape[0] // batch
    w = e.shape[1]
    e3 = e.reshape(batch, nc, w)
    xin, xlast = pl.pallas_call(
        _s5_scan_kernel,
        grid=(batch,),
        in_specs=[pl.BlockSpec((None, nc, w), lambda b: (b, 0, 0)), pl.BlockSpec((1, w), lambda b: (0, 0))],
        out_specs=[pl.BlockSpec((None, nc, w), lambda b: (b, 0, 0)), pl.BlockSpec((None, 1, w), lambda b: (b, 0, 0))],
        out_shape=[jax.ShapeDtypeStruct((batch, nc, w), F32), jax.ShapeDtypeStruct((batch, 1, w), F32)],
        compiler_params=_cparams(("parallel",)),
        name="s5_scan",
    )(e3, a_chunk)
    return xin.reshape(batch * nc, w), xlast.reshape(batch, w)


def _s5_step_kernel(e_ref, a_ref, x0_ref, o_ref):
    gp = S5_GP
    are, aim = a_ref[:, :gp], a_ref[:, gp:]
    xr, xi = x0_ref[:, :gp], x0_ref[:, gp:]
    o_ref[:, :gp] = are * xr - aim * xi + e_ref[:, :gp]
    o_ref[:, gp:] = are * xi + aim * xr + e_ref[:, gp:]


def _s5_step(e, a_chunk, x0):
    return pl.pallas_call(
        _s5_step_kernel,
        out_shape=jax.ShapeDtypeStruct(e.shape, F32),
        name="s5_step",
    )(e, a_chunk, x0)


def _s5_matrices(lam_re, lam_im, b_re, b_im, c_re, c_im, log_dt, chunk):
    hp = lax.Precision.HIGHEST
    g, p, cc, L = S5_GROUPS, S5_P, S5_GROUP, chunk
    dt = jnp.exp(log_dt)[:, None]
    zr, zi = lam_re * dt, lam_im * dt
    d = jnp.arange(L + 1, dtype=F32)[:, None, None]
    mag = jnp.exp(zr[None] * d)
    pw_r, pw_i = mag * jnp.cos(zi[None] * d), mag * jnp.sin(zi[None] * d)
    ar, ai = pw_r[1] - 1.0, pw_i[1]
    den = lam_re * lam_re + lam_im * lam_im
    qr, qi = (ar * lam_re + ai * lam_im) / den, (ai * lam_re - ar * lam_im) / den
    bb_r = qr[..., None] * b_re - qi[..., None] * b_im
    bb_i = qr[..., None] * b_im + qi[..., None] * b_re
    def spread(a):
        x = a.shape[-1]
        lane_group = jnp.arange(g * x, dtype=jnp.int32) // x
        own = lane_group[None, None, None, :] == jnp.arange(g, dtype=jnp.int32)[None, :, None, None]
        return jnp.where(own, jnp.tile(a, (1, 1, 1, g)), 0.0)

    rev_r, rev_i = pw_r[L - 1::-1], pw_i[L - 1::-1]
    e_r = rev_r[..., None] * bb_r[None] - rev_i[..., None] * bb_i[None]
    e_i = rev_r[..., None] * bb_i[None] + rev_i[..., None] * bb_r[None]

    def bend_of(e):
        return spread(jnp.transpose(e, (0, 1, 3, 2))).reshape(L * g * cc, g * p)

    bend = jnp.concatenate([bend_of(e_r), bend_of(e_i)], axis=1)
    cp_r = c_re[None] * pw_r[:L, :, None, :] - c_im[None] * pw_i[:L, :, None, :]
    cp_i = c_re[None] * pw_i[:L, :, None, :] + c_im[None] * pw_r[:L, :, None, :]
    kd = (jnp.einsum('dgcp,gpe->dgce', cp_r, bb_r, precision=hp)
          - jnp.einsum('dgcp,gpe->dgce', cp_i, bb_i, precision=hp))
    kbd = spread(jnp.transpose(kd, (0, 1, 3, 2)))
    zero_blk = jnp.zeros((1,) + kbd.shape[1:], F32)
    cols = [jnp.concatenate([kbd[t::-1]] + [zero_blk] * (L - 1 - t), axis=0).reshape(L * g * cc, g * cc)
            for t in range(L)]
    mtoep = jnp.concatenate(cols, axis=1)
    c1_r = c_re[None] * pw_r[1:, :, None, :] - c_im[None] * pw_i[1:, :, None, :]
    c1_i = c_re[None] * pw_i[1:, :, None, :] + c_im[None] * pw_r[1:, :, None, :]

    def ccar_of(c1):
        per_t = spread(jnp.transpose(c1, (0, 1, 3, 2)))
        return jnp.concatenate([per_t[t].reshape(g * p, g * cc) for t in range(L)], axis=1)

    ccar = jnp.concatenate([ccar_of(c1_r), ccar_of(-c1_i)], axis=0)
    a_chunk = jnp.concatenate([pw_r[L].reshape(1, g * p), pw_i[L].reshape(1, g * p)], axis=1)
    return bend.astype(BF16), mtoep.astype(BF16), ccar.astype(BF16), a_chunk


def _gla_kernel(*refs, lb, n_blocks, n_chains, has_s0):
    nc = n_chains
    qk_refs, v_refs, bc_refs, gg_refs = (refs[i * nc:(i + 1) * nc] for i in range(4))
    gn_ref = refs[4 * nc]
    pos = 4 * nc + 1
    s0_ref = refs[pos] if has_s0 else None
    pos += int(has_s0)
    o_refs = refs[pos:pos + nc]
    s_out_ref, st_ref, obuf_ref = refs[pos + nc:]
    t = pl.program_id(1)
    kw, vw = GLA_QK, C_WIDTH
    def same_head(shape, row_size, col_size):
        r = _group_of(lax.broadcasted_iota(jnp.int32, shape, 0), row_size)
        c = _group_of(lax.broadcasted_iota(jnp.int32, shape, 1), col_size)
        return r == c

    bd_mask = same_head((vw, kw), GLA_DV, GLA_DK)

    @pl.when(t == 0)
    def _():
        if has_s0:
            for c in range(nc):
                s0t = s0_ref[c].T
                st_ref[c] = jnp.where(bd_mask, jnp.concatenate([s0t] * GLA_HEADS, axis=0), 0.0)
        else:
            st_ref[...] = jnp.zeros(st_ref.shape, F32)

    head_expand = jnp.where(same_head((kw, vw), GLA_DK, GLA_DV), 1.0, 0.0).astype(BF16)
    row = lax.broadcasted_iota(jnp.int32, (lb, kw), 0)
    scale = GLA_DK ** -0.5

    def block(c, r0):
        q = qk_refs[c][pl.ds(r0, lb), 0:kw] * scale
        k = qk_refs[c][pl.ds(r0, lb), kw:2 * kw]
        v = v_refs[c][pl.ds(r0, lb), :]
        bc = bc_refs[c][pl.ds(r0, lb), :]
        bl = bc[lb - 1:lb, :]
        st = st_ref[c]
        o = _dot_nt((q * jnp.exp(bc)).astype(BF16), st.astype(BF16))
        parts = []
        for s in range(lb):
            dec = jnp.exp(jnp.minimum(bc - bc[s:s + 1, :], 0.0))
            parts.append(jnp.where(row >= s, q * k[s:s + 1, :] * dec, 0.0))
        att = _dot(jnp.concatenate(parts, axis=0).astype(BF16), head_expand)
        for s in range(lb):
            o = o + att[s * lb:(s + 1) * lb, :] * v[s:s + 1, :]
        obuf_ref[c, pl.ds(r0, lb), :] = o
        ke = (k * jnp.exp(bl - bc)).astype(BF16)
        upd = _dot_tn(v.astype(BF16), ke)
        st_ref[c] = jnp.exp(bl) * st + jnp.where(bd_mask, upd, 0.0)

    def body(i, carry):
        r0 = pl.multiple_of(i * lb, lb)
        for c in range(nc):
            block(c, r0)
        return carry

    lax.fori_loop(0, n_blocks, body, 0)

    seg_ones = jnp.where(same_head((vw, vw), GLA_DV, GLA_DV), 1.0, 0.0).astype(BF16)
    for c in range(nc):
        o = obuf_ref[c]
        ms = _dot((o * o).astype(BF16), seg_ones) * (1.0 / GLA_DV)
        gg = gg_refs[c][...]
        o_refs[c][...] = (o * lax.rsqrt(ms + RMS_EPS) * gn_ref[...] * (gg * _sigmoid(gg))).astype(BF16)

    @pl.when(t == pl.num_programs(1) - 1)
    def _():
        for c in range(nc):
            st = st_ref[c]
            comp = st[0:GLA_DV]
            for hd in range(1, GLA_HEADS):
                comp = comp + st[hd * GLA_DV:(hd + 1) * GLA_DV]
            s_out_ref[c] = comp.T


def _gla(qkg, vg, bc, gg, gn_vec, s0, batch, seq, lb, tile, n_chains):
    tile = min(tile, seq)
    nt = seq // tile
    nc = n_chains
    assert batch % nc == 0

    def row(w, c):
        return pl.BlockSpec((tile, w), lambda g, t: ((g * nc + c) * nt + t, 0))

    st_spec = pl.BlockSpec((nc, GLA_QK, GLA_DV), lambda g, t: (g, 0, 0))
    in_specs, args = [], []
    for arr, w in ((qkg, 2 * GLA_QK), (vg, C_WIDTH), (bc, GLA_QK), (gg, C_WIDTH)):
        in_specs += [row(w, c) for c in range(nc)]
        args += [arr] * nc
    in_specs.append(pl.BlockSpec((1, C_WIDTH), lambda g, t: (0, 0)))
    args.append(gn_vec)
    if s0 is not None:
        in_specs.append(st_spec)
        args.append(s0)
    res = pl.pallas_call(
        functools.partial(_gla_kernel, lb=lb, n_blocks=tile // lb, n_chains=nc, has_s0=s0 is not None),
        grid=(batch // nc, nt),
        in_specs=in_specs,
        out_specs=[pl.BlockSpec((tile, C_WIDTH), lambda g, t: (g * nt + t, 0))] * nc + [st_spec],
        out_shape=[jax.ShapeDtypeStruct((batch // nc * seq, C_WIDTH), BF16)] * nc
        + [jax.ShapeDtypeStruct((batch, GLA_QK, GLA_DV), F32)],
        scratch_shapes=[pltpu.VMEM((nc, C_WIDTH, GLA_QK), F32), pltpu.VMEM((nc, tile, C_WIDTH), F32)],
        compiler_params=_cparams(("parallel", "arbitrary")),
        name="gla",
    )(*args)
    return res[:nc], res[nc]


def _outproj_kernel(x_ref, oa_ref, yc_ref, u_ref, d_ref, gw_ref, gb_ref, oc_ref, w_ref, gt_ref, gpost_ref, o_ref,
                    ya_ref, yb_ref, *, s5_chunk):
    tm = x_ref.shape[0]
    for s in range(s5_chunk):
        for half, part_ref in enumerate((ya_ref, yb_ref)):
            c0 = s * B_WIDTH + half * V7X_LANES
            part_ref[pl.ds(s, tm // s5_chunk, stride=s5_chunk), :] = yc_ref[:, c0:c0 + V7X_LANES]
    y = jnp.concatenate([ya_ref[...], yb_ref[...]], axis=1) + d_ref[...] * u_ref[...]
    g = 0.5 * y * (1.0 + jnp.tanh(math.sqrt(2.0 / math.pi) * (y + 0.044715 * (y * y * y))))
    ob = g * _sigmoid(_dot(g.astype(BF16), gw_ref[...]) + gb_ref[...])
    a0, a1 = A_WIDTH, A_WIDTH + B_WIDTH
    mixed = (_dot(oa_ref[...], w_ref[0:a0, :]) + _dot(ob.astype(BF16), w_ref[a0:a1, :])
             + _dot(oc_ref[...], w_ref[a1:, :]))
    o_ref[...] = x_ref[...] + (1.0 + gt_ref[...]) * _rms(mixed, gpost_ref[...])


def _outproj(x, oa, yc, u, s5_d, glu_w, glu_b, oc, w_out, mod, layer, norm_post, tm, s5_chunk):
    n, d = x.shape
    row = lambda w: pl.BlockSpec((tm, w), lambda i: (i, 0))
    lvec = lambda w: pl.BlockSpec((None, 1, w), lambda i: (layer, 0, 0))
    return pl.pallas_call(
        functools.partial(_outproj_kernel, s5_chunk=s5_chunk),
        scratch_shapes=[pltpu.VMEM((tm, V7X_LANES), F32), pltpu.VMEM((tm, V7X_LANES), F32)],
        grid=(n // tm,),
        in_specs=[
            row(d), row(A_WIDTH), pl.BlockSpec((tm // s5_chunk, s5_chunk * B_WIDTH), lambda i: (i, 0)),
            row(B_WIDTH), lvec(B_WIDTH),
            _resident((None, B_WIDTH, B_WIDTH), lambda i: (layer, 0, 0)), lvec(B_WIDTH),
            row(C_WIDTH), _resident((None, d, d), lambda i: (layer, 0, 0)),
            mod.spec(1, 2, tm), pl.BlockSpec((None, None, 1, d), lambda i: (layer, 1, 0, 0)),
        ],
        out_specs=row(d),
        out_shape=jax.ShapeDtypeStruct((n, d), F32),
        compiler_params=_cparams(("parallel",)),
        name="outproj",
    )(x, oa, yc, u, s5_d, glu_w, glu_b, oc, w_out, mod.arr, norm_post)


def kernel(x_prompt, x_sample, cache_k, cache_v, state_s5_re, state_s5_im, state_gla, page_table, c_prompt, c_sample, ada_w, ada_b, norm_pre, norm_post, ffn1_wi, ffn1_wo, ffn2_wi, ffn2_wo, w_in, w_out, lam_q1, lam_k1, lam_q2, lam_k2, subln_g, s5_lam_re, s5_lam_im, s5_b_re, s5_b_im, s5_c_re, s5_c_im, s5_d, s5_log_dt, s5_glu_w, s5_glu_b, gla_gate_w2, gla_gate_b, gla_norm_g):
    depth = ada_w.shape[0]
    batch, seq, d = x_prompt.shape
    dec_b, dec_t, _ = x_sample.shape
    n_pages = page_table.shape[1]
    page = cache_k.shape[2]
    past_len = n_pages * page
    n_p, n_s = batch * seq, dec_b * dec_t
    tm_p, tm_s = min(ROW_TILE, seq), min(ROW_TILE, n_s)
    tf_p = min(FFN_TILE, seq)

    n_c = batch + dec_b
    pad_c = (-n_c) % 8
    c_all = jnp.concatenate([c_prompt, c_sample, jnp.zeros((pad_c, d), F32)], axis=0)
    mod_all = _ada_mod(c_all, ada_w, ada_b)
    mod_p = mod_all[:, :batch].reshape(depth, batch, 1, N_SUB * 3 * d)
    mod_s = jnp.repeat(mod_all[:, batch:n_c], dec_t, axis=1)

    bf = lambda a: a.astype(BF16)
    wi1, wo1, wi2, wo2, w_out_b = bf(ffn1_wi), bf(ffn1_wo), bf(ffn2_wi), bf(ffn2_wo), bf(w_out)
    in_dim = w_in.shape[2]
    w_in_b = bf(jnp.pad(w_in, ((0, 0), (0, 0), (0, V7X_LANES - GLA_RANK))))
    assert w_in_b.shape[2] == in_dim - GLA_RANK + V7X_LANES
    w2p = bf(jnp.pad(gla_gate_w2, ((0, 0), (0, V7X_LANES - GLA_RANK), (0, 0))))
    b2 = gla_gate_b.reshape(depth, 1, GLA_QK)
    glu_w = bf(s5_glu_w)
    glu_b = s5_glu_b.reshape(depth, 1, B_WIDTH)
    s5_d3 = s5_d.reshape(depth, 1, B_WIDTH)
    npre = norm_pre.reshape(depth, N_SUB, 1, d)
    npost = norm_post.reshape(depth, N_SUB, 1, d)
    gn_vec = jnp.tile(gla_norm_g, (1, GLA_HEADS)).reshape(depth, 1, C_WIDTH)

    tabs_p = _rope_tables(jnp.arange(seq, dtype=jnp.int32))
    tabs_s = _rope_tables(past_len + jnp.arange(n_s, dtype=jnp.int32) % dec_t)

    cache_k4 = cache_k.reshape(depth, cache_k.shape[1], page * A_HEADS, A_HD)
    cache_v4 = cache_v.reshape(depth, cache_v.shape[1], page * A_HEADS, A_HD)

    def pad_tokens(a, edge=False):
        a3 = a.reshape(dec_b, dec_t, a.shape[-1])
        return jnp.pad(a3, ((0, 0), (0, DEC_PAD - dec_t), (0, 0)), mode='edge' if edge else 'constant')

    yp = x_prompt.reshape(n_p, d)
    ys = x_sample.reshape(n_s, d)
    outs_p, outs_s = [], []
    for l in range(depth):
        mp = _Mod(mod_p, l, seq, d)
        ms = _Mod(mod_s, l, n_s, d)
        lam_init = 0.8 - 0.6 * math.exp(-0.3 * l)
        lam = (jnp.exp(jnp.sum(lam_q1[l] * lam_k1[l])) - jnp.exp(jnp.sum(lam_q2[l] * lam_k2[l])) + lam_init)
        lam_vec = jnp.full((1, A_HD), lam, F32)
        g_eff = (subln_g[l] * (1.0 - lam_init)).reshape(1, A_HD)

        yp = _ffn(yp, mp, 0, l, npre, npost, wi1, wo1, tf_p)
        ys = _ffn(ys, ms, 0, l, npre, npost, wi1, wo1, tm_s)

        lp = min(S5_CHUNK_PROMPT, seq)
        (qat_p, qbt_p, k32_p, kbf_p, v32_p, vt_p, u_p, uc_p, qkg_p, vg_p, gg_p, bc_p) = _inproj(
            yp, mp, l, npre, w_in_b, w2p, b2, tabs_p, batch, tm_p, GLA_BLOCK_PROMPT, lp, True)
        (qa_s, qb_s, k32_s, v32_s, u_s, uc_s, qkg_s, vg_s, gg_s, bc_s) = _inproj(
            ys, ms, l, npre, w_in_b, w2p, b2, tabs_s, 1, tm_s, dec_t, dec_t, False)

        oa_p = _attn_prompt(qat_p, qbt_p, kbf_p, vt_p, lam_vec, g_eff, batch, seq)

        def pad_new_rows(a):
            a3 = a.reshape(dec_b, dec_t * A_HEADS, A_HD)
            return jnp.pad(a3, ((0, 0), (0, (DEC_PAD - dec_t) * A_HEADS), (0, 0)))

        oa_s8 = _attn_decode(pad_tokens(qa_s), pad_tokens(qb_s), pad_new_rows(k32_s), pad_new_rows(v32_s),
                             cache_k4, cache_v4, page_table, l, lam_vec, g_eff, dec_t)
        oa_s = oa_s8[:, :dec_t].reshape(n_s, A_WIDTH).astype(BF16)

        s5p = (s5_lam_re[l], s5_lam_im[l], s5_b_re[l], s5_b_im[l], s5_c_re[l], s5_c_im[l], s5_log_dt[l])
        bend, mtoep, ccar, a_chunk = _s5_matrices(*s5p, lp)
        e_p = _mm([(uc_p, bend)])
        xin_p, xlast_p = _s5_scan(e_p, a_chunk, batch)
        yc_p = _mm([(uc_p, mtoep), (xin_p, ccar)])
        bend4, mtoep4, ccar4, a_chunk4 = _s5_matrices(*s5p, dec_t)
        x0 = jnp.concatenate([state_s5_re[l].reshape(dec_b, S5_GP), state_s5_im[l].reshape(dec_b, S5_GP)], axis=1)
        xlast_s = _s5_step(_mm([(uc_s, bend4)]), a_chunk4, x0)
        yc_s = _mm([(uc_s, mtoep4), (x0, ccar4)])

        ncp = math.gcd(batch, GLA_CHAINS_PROMPT)
        oc_parts, gla_p = _gla(qkg_p, vg_p, bc_p, gg_p, gn_vec[l], None, batch, seq, GLA_BLOCK_PROMPT, GLA_TILE, ncp)
        oc_p = jnp.stack([o.reshape(batch // ncp, seq, C_WIDTH) for o in oc_parts], axis=1).reshape(n_p, C_WIDTH)
        flat8 = lambda a, edge=False: pad_tokens(a, edge).reshape(dec_b * DEC_PAD, a.shape[-1])
        ncs = math.gcd(dec_b, GLA_CHAINS_DECODE)
        oc_parts, gla_s = _gla(flat8(qkg_s), flat8(vg_s), flat8(bc_s, True), flat8(gg_s), gn_vec[l],
                               state_gla[l].reshape(dec_b, GLA_QK, GLA_DV), dec_b, DEC_PAD, DEC_PAD, DEC_PAD, ncs)
        oc_s8 = jnp.stack([o.reshape(dec_b // ncs, DEC_PAD, C_WIDTH) for o in oc_parts], axis=1)
        oc_s = oc_s8.reshape(dec_b, DEC_PAD, C_WIDTH)[:, :dec_t].reshape(n_s, C_WIDTH)

        yp = _outproj(yp, oa_p, yc_p, u_p, s5_d3, glu_w, glu_b, oc_p, w_out_b, mp, l, npost, tm_p, lp)
        ys = _outproj(ys, oa_s, yc_s, u_s, s5_d3, glu_w, glu_b, oc_s, w_out_b, ms, l, npost, tm_s, dec_t)

        yp = _ffn(yp, mp, 2, l, npre, npost, wi2, wo2, tf_p)
        ys = _ffn(ys, ms, 2, l, npre, npost, wi2, wo2, tm_s)

        outs_p.append((k32_p.reshape(batch, seq, A_HEADS, A_HD), v32_p.reshape(batch, seq, A_HEADS, A_HD),
                       xlast_p[:, :S5_GP].reshape(batch, S5_GROUPS, S5_P),
                       xlast_p[:, S5_GP:].reshape(batch, S5_GROUPS, S5_P),
                       gla_p.reshape(batch, GLA_HEADS, GLA_DK, GLA_DV)))
        outs_s.append((k32_s.reshape(dec_b, dec_t, A_HEADS, A_HD), v32_s.reshape(dec_b, dec_t, A_HEADS, A_HD),
                       xlast_s[:, :S5_GP].reshape(dec_b, S5_GROUPS, S5_P),
                       xlast_s[:, S5_GP:].reshape(dec_b, S5_GROUPS, S5_P),
                       gla_s.reshape(dec_b, GLA_HEADS, GLA_DK, GLA_DV)))

    kp, vp, srp, sip, gp = [jnp.stack(z) for z in zip(*outs_p)]
    ks, vs, srs, sis, gs = [jnp.stack(z) for z in zip(*outs_s)]
    return (yp.reshape(batch, seq, d), ys.reshape(dec_b, dec_t, d), kp, vp, srp, sip, gp, ks, vs, srs, sis, gs)
```

```python
import functools
import math

import jax
import jax.numpy as jnp
import numpy as np
from jax import lax
from jax.experimental import pallas as pl
from jax.experimental.pallas import tpu as pltpu

F32 = jnp.float32
BF16 = jnp.bfloat16

A_HEADS = 4
A_DH = 64
A_HD = 2 * A_DH
A_WIDTH = A_HEADS * A_HD
ROT_DIM = 16
ROPE_THETA = 500000.0
NEG_INF = -1e30
LOG2_E = math.log2(math.e)
S5_GROUPS = 16
S5_GROUP = 16
S5_P = 64
B_WIDTH = S5_GROUPS * S5_GROUP
S5_GP = S5_GROUPS * S5_P
GLA_HEADS = 4
GLA_DK = 32
GLA_DV = 64
GLA_QK = GLA_HEADS * GLA_DK
C_WIDTH = GLA_HEADS * GLA_DV
GLA_RANK = 16
GLA_TAU = 16.0
MACARON_W = 0.5
RMS_EPS = 1e-6
N_SUB = 3

V7X_LANES = 128
V7X_VMEM_LIMIT_BYTES = 56 * 1024 * 1024

ROW_TILE = 512
FFN_TILE = 1024
FFN_CHUNK = 256
ATTN_TQ = 1024
ATTN_TK = 1024
S5_CHUNK_PROMPT = 8
GLA_BLOCK_PROMPT = 32
GLA_TILE = 512
GLA_CHAINS_PROMPT = 2
GLA_CHAINS_DECODE = 8
DEC_PAD = 8
DECODE_SEQS_PER_STEP = 2


def _cparams(sem):
    return pltpu.CompilerParams(dimension_semantics=sem, vmem_limit_bytes=V7X_VMEM_LIMIT_BYTES)


def _resident(shape, index_map):
    return pl.BlockSpec(shape, index_map, pipeline_mode=pl.Buffered(1))


def _rms(x, g):
    return x * lax.rsqrt(jnp.mean(x * x, axis=-1, keepdims=True) + RMS_EPS) * g


def _sigmoid(x):
    return 1.0 / (1.0 + jnp.exp(-x))


def _group_of(idx, size):
    assert size & (size - 1) == 0
    return jnp.right_shift(idx, size.bit_length() - 1)


def _dot(a, b):
    return jnp.dot(a, b, preferred_element_type=F32)


def _dot_nt(a, b):
    return lax.dot_general(a, b, (((1,), (1,)), ((), ())), preferred_element_type=F32)


def _dot_tn(a, b):
    return lax.dot_general(a, b, (((0,), (0,)), ((), ())), preferred_element_type=F32)


def _ada_kernel(c_ref, w_ref, b_ref, o_ref):
    c = c_ref[...]
    a = (c * _sigmoid(c)).astype(BF16)
    o_ref[...] = _dot(a, w_ref[...].astype(BF16)) + b_ref[...]


def _ada_mod(c_all, ada_w, ada_b):
    depth, d, n = ada_w.shape
    rows = c_all.shape[0]
    tn = n // 4
    return pl.pallas_call(
        _ada_kernel,
        grid=(depth, n // tn),
        in_specs=[
            pl.BlockSpec((rows, d), lambda l, j: (0, 0)),
            pl.BlockSpec((None, d, tn), lambda l, j: (l, 0, j)),
            pl.BlockSpec((None, 1, tn), lambda l, j: (l, 0, j)),
        ],
        out_specs=pl.BlockSpec((None, rows, tn), lambda l, j: (l, 0, j)),
        out_shape=jax.ShapeDtypeStruct((depth, rows, n), F32),
        compiler_params=_cparams(("arbitrary", "arbitrary")),
        name="ada_mod",
    )(c_all, ada_w, ada_b.reshape(depth, 1, n))


class _Mod:
    def __init__(self, arr, layer, rows_per_seq, d):
        self.arr = arr
        self.layer = layer
        self.d = d
        self.per_row = arr.ndim == 3
        self.rows_per_seq = rows_per_seq

    def spec(self, sub, kind, tm):
        col = N_SUB * sub + kind
        l = self.layer
        if self.per_row:
            return pl.BlockSpec((None, tm, self.d), lambda i: (l, i, col))
        tps = self.rows_per_seq // tm
        return pl.BlockSpec((None, None, 1, self.d), lambda i: (l, i // tps, 0, col))


def _ffn_kernel(x_ref, sh_ref, sc_ref, gt_ref, gpre_ref, gpost_ref, wi_ref, wo_ref, o_ref, *, d_ff, chunk):
    x = x_ref[...]
    h = (_rms(x, gpre_ref[...]) * (1.0 + sc_ref[...]) + sh_ref[...]).astype(BF16)
    acc = jnp.zeros(x.shape, F32)
    for c in range(d_ff // chunk):
        lo = c * chunk
        g = _dot(h, wi_ref[:, lo:lo + chunk])
        u = _dot(h, wi_ref[:, d_ff + lo:d_ff + lo + chunk])
        a = (g * _sigmoid(g) * u).astype(BF16)
        acc = acc + _dot(a, wo_ref[lo:lo + chunk, :])
    o_ref[...] = x + MACARON_W * (1.0 + gt_ref[...]) * _rms(acc, gpost_ref[...])


def _ffn(x, mod, sub, layer, norm_pre, norm_post, wi, wo, tm):
    n, d = x.shape
    d_ff = wo.shape[1]
    vec = pl.BlockSpec((None, None, 1, d), lambda i: (layer, sub, 0, 0))
    return pl.pallas_call(
        functools.partial(_ffn_kernel, d_ff=d_ff, chunk=FFN_CHUNK),
        grid=(n // tm,),
        in_specs=[
            pl.BlockSpec((tm, d), lambda i: (i, 0)),
            mod.spec(sub, 0, tm), mod.spec(sub, 1, tm), mod.spec(sub, 2, tm),
            vec, vec,
            _resident((None, d, 2 * d_ff), lambda i: (layer, 0, 0)),
            _resident((None, d_ff, d), lambda i: (layer, 0, 0)),
        ],
        out_specs=pl.BlockSpec((tm, d), lambda i: (i, 0)),
        out_shape=jax.ShapeDtypeStruct((n, d), F32),
        compiler_params=_cparams(("parallel",)),
        name="ffn",
    )(x, mod.arr, mod.arr, mod.arr, norm_pre, norm_post, wi, wo)


def _inproj_kernel(x_ref, sh_ref, sc_ref, gpre_ref, w_ref, w2_ref, b2_ref, tc_ref, ts1_ref, ts2_ref, *refs,
                   gla_block, s5_chunk, transposed):
    out_refs, (ua_ref, ub_ref) = refs[:-2], refs[-2:]
    if transposed:
        qa_ref, qb_ref, k32_ref, kbf_ref, v32_ref, vt_ref, u_ref, uc_ref, qkg_ref, vg_ref, gg_ref, bc_ref = out_refs
    else:
        qa_ref, qb_ref, k32_ref, v32_ref, u_ref, uc_ref, qkg_ref, vg_ref, gg_ref, bc_ref = out_refs
    x = x_ref[...]
    tm = x.shape[0]
    h = (_rms(x, gpre_ref[...]) * (1.0 + sc_ref[...]) + sh_ref[...]).astype(BF16)
    tc, ts1, ts2 = tc_ref[...], ts1_ref[...], ts2_ref[...]
    if transposed:
        first_comp = lax.broadcasted_iota(jnp.int32, (A_HD, tm), 0) < A_DH
    else:
        first_comp = lax.broadcasted_iota(jnp.int32, (tm, A_HD), 1) < A_DH

    def rope(z):
        up = pltpu.roll(z, A_HD - ROT_DIM // 2, axis=1)
        dn = pltpu.roll(z, ROT_DIM // 2, axis=1)
        return z * tc + up * ts1 + dn * ts2

    w = A_WIDTH
    for hd in range(A_HEADS):
        lo = hd * A_HD
        if hd % 2 == 0:
            q2 = _dot(h, w_ref[:, lo:lo + 2 * A_HD])
            k2 = _dot(h, w_ref[:, w + lo:w + lo + 2 * A_HD])
            v2 = _dot(h, w_ref[:, 2 * w + lo:2 * w + lo + 2 * A_HD])
        part = slice((hd % 2) * A_HD, (hd % 2 + 1) * A_HD)
        q = rope(q2[:, part]) * (A_DH ** -0.5 * LOG2_E)
        k = rope(k2[:, part])
        v = v2[:, part]
        k32_ref[pl.ds(hd, tm, stride=A_HEADS), :] = k
        v32_ref[pl.ds(hd, tm, stride=A_HEADS), :] = v
        if transposed:
            qt = q.T
            qa_ref[lo:lo + A_HD, :] = jnp.where(first_comp, qt, 0.0).astype(BF16)
            qb_ref[lo:lo + A_HD, :] = jnp.where(first_comp, 0.0, qt).astype(BF16)
            kbf_ref[:, lo:lo + A_HD] = k.astype(BF16)
            vt_ref[lo:lo + A_HD, :] = v.T.astype(BF16)
        else:
            qa_ref[:, lo:lo + A_HD] = jnp.where(first_comp, q, 0.0)
            qb_ref[:, lo:lo + A_HD] = jnp.where(first_comp, 0.0, q)
    off = 3 * w
    u = _dot(h, w_ref[:, off:off + B_WIDTH])
    u_ref[...] = u
    ua_ref[...] = u[:, :V7X_LANES]
    ub_ref[...] = u[:, V7X_LANES:]
    for s in range(s5_chunk):
        for half, part_ref in enumerate((ua_ref, ub_ref)):
            c0 = s * B_WIDTH + half * V7X_LANES
            uc_ref[:, c0:c0 + V7X_LANES] = part_ref[pl.ds(s, tm // s5_chunk, stride=s5_chunk), :]
    off += B_WIDTH
    qkg_ref[...] = _dot(h, w_ref[:, off:off + 2 * GLA_QK])
    off += 2 * GLA_QK
    vg_ref[...] = _dot(h, w_ref[:, off:off + C_WIDTH])
    off += C_WIDTH
    gg_ref[...] = _dot(h, w_ref[:, off:off + C_WIDTH])
    off += C_WIDTH
    rg = _dot(h, w_ref[:, off:off + V7X_LANES]).astype(BF16)
    gate = _dot(rg, w2_ref[...]) + b2_ref[...]
    log_a = (jnp.minimum(gate, 0.0) - jnp.log(1.0 + jnp.exp(-jnp.abs(gate)))) / GLA_TAU
    r = lax.broadcasted_iota(jnp.int32, (tm, tm), 0)
    c = lax.broadcasted_iota(jnp.int32, (tm, tm), 1)
    same_block = _group_of(r, gla_block) == _group_of(c, gla_block)
    tri = jnp.where(same_block, jnp.where(c <= r, 1.0, 0.0), 0.0).astype(BF16)
    hi = log_a.astype(BF16)
    lo_part = (log_a - hi.astype(F32)).astype(BF16)
    bc_ref[...] = _dot(tri, hi) + _dot(tri, lo_part)


def _inproj(x, mod, layer, norm_pre, w_in, w2p, b2, tabs, batch, tm, gla_block, s5_chunk, transposed):
    n, d = x.shape
    n_cols = w_in.shape[2]
    seq = n // batch
    tps = seq // tm if transposed else None
    tab_tiles = tabs[0].shape[0] // tm
    row = lambda width: pl.BlockSpec((tm, width), lambda i: (i, 0))
    tab = pl.BlockSpec((tm, V7X_LANES), lambda i: (i % tab_tiles, 0))
    f32_row = lambda width: (row(width), jax.ShapeDtypeStruct((n, width), F32))
    head_rows = (pl.BlockSpec((tm * A_HEADS, A_HD), lambda i: (i, 0)), jax.ShapeDtypeStruct((n * A_HEADS, A_HD), F32))
    chunk_rows = (pl.BlockSpec((tm // s5_chunk, s5_chunk * B_WIDTH), lambda i: (i, 0)),
                  jax.ShapeDtypeStruct((n // s5_chunk, s5_chunk * B_WIDTH), F32))
    if transposed:
        tr = (pl.BlockSpec((None, A_WIDTH, tm), lambda i: (i // tps, 0, i % tps)),
              jax.ShapeDtypeStruct((batch, A_WIDTH, seq), BF16))
        outs = [tr, tr, head_rows, (row(A_WIDTH), jax.ShapeDtypeStruct((n, A_WIDTH), BF16)), head_rows, tr]
    else:
        outs = [f32_row(A_WIDTH), f32_row(A_WIDTH), head_rows, head_rows]
    outs += [f32_row(B_WIDTH), chunk_rows, f32_row(2 * GLA_QK), f32_row(C_WIDTH), f32_row(C_WIDTH), f32_row(GLA_QK)]
    return pl.pallas_call(
        functools.partial(_inproj_kernel, gla_block=gla_block, s5_chunk=s5_chunk, transposed=transposed),
        scratch_shapes=[pltpu.VMEM((tm, V7X_LANES), F32), pltpu.VMEM((tm, V7X_LANES), F32)],
        grid=(n // tm,),
        in_specs=[
            row(d), mod.spec(1, 0, tm), mod.spec(1, 1, tm),
            pl.BlockSpec((None, None, 1, d), lambda i: (layer, 1, 0, 0)),
            _resident((None, d, n_cols), lambda i: (layer, 0, 0)),
            _resident((None, V7X_LANES, GLA_QK), lambda i: (layer, 0, 0)),
            pl.BlockSpec((None, 1, GLA_QK), lambda i: (layer, 0, 0)),
            tab, tab, tab,
        ],
        out_specs=[spec for spec, _ in outs],
        out_shape=[shape for _, shape in outs],
        compiler_params=_cparams(("parallel",)),
        name="inproj",
    )(x, mod.arr, mod.arr, norm_pre, w_in, w2p, b2, *tabs)


def _rope_tables(pos):
    half = ROT_DIM // 2
    inv = ROPE_THETA ** (-jnp.arange(half, dtype=F32) / half)
    ang = pos.astype(F32)[:, None] * inv[None, :]
    cos, sin = jnp.cos(ang), jnp.sin(ang)
    lane = np.arange(V7X_LANES) % A_DH
    idx = lane % half
    is_lo = jnp.asarray(lane < half)
    is_hi = jnp.asarray((lane >= half) & (lane < ROT_DIM))
    cos_l, sin_l = cos[:, idx], sin[:, idx]
    tc = jnp.where(is_lo | is_hi, cos_l, 1.0)
    ts1 = jnp.where(is_lo, -sin_l, 0.0)
    ts2 = jnp.where(is_hi, sin_l, 0.0)
    return tc, ts1, ts2


def _attn_kernel(qi_tab, ki_tab, qa_ref, qb_ref, k_ref, vt_ref, lam_ref, g_ref, o_ref, m_ref, l_ref, acc_ref,
                 *, tq, tk):
    p = pl.program_id(2)
    qi = qi_tab[p]
    ki = ki_tab[p]
    q0 = qi * tq
    k0 = ki * tk

    @pl.when(ki == 0)
    def _():
        m_ref[...] = jnp.full(m_ref.shape, NEG_INF, F32)
        l_ref[...] = jnp.zeros(l_ref.shape, F32)
        acc_ref[...] = jnp.zeros(acc_ref.shape, F32)

    def update(c, q_ref, rows, cols, masked):
        s = _dot(k_ref[rows, :], q_ref[:, cols])
        if masked:
            key = k0 + rows.start + lax.broadcasted_iota(jnp.int32, s.shape, 0)
            qry = q0 + cols.start + lax.broadcasted_iota(jnp.int32, s.shape, 1)
            s = jnp.where(key <= qry, s, NEG_INF)
        m_prev = m_ref[c, :, cols]
        m_new = jnp.maximum(m_prev, jnp.max(s, axis=0, keepdims=True))
        alpha = jnp.exp2(m_prev - m_new)
        pr = jnp.exp2(s - m_new)
        l_ref[c, :, cols] = alpha * l_ref[c, :, cols] + jnp.sum(pr, axis=0, keepdims=True)
        acc_ref[c, :, cols] = alpha * acc_ref[c, :, cols] + _dot(vt_ref[:, rows], pr.astype(BF16))
        m_ref[c, :, cols] = m_new

    def step(masked):
        for c, q_ref in enumerate((qa_ref, qb_ref)):
            if masked and tq == tk:
                half = tq // 2
                update(c, q_ref, slice(0, half), slice(0, half), True)
                update(c, q_ref, slice(0, tk), slice(half, tq), True)
            else:
                update(c, q_ref, slice(0, tk), slice(0, tq), masked)

    fully_visible = k0 + (tk - 1) <= q0

    @pl.when(fully_visible)
    def _():
        step(False)

    @pl.when(jnp.logical_not(fully_visible))
    def _():
        step(True)

    @pl.when(k0 + tk >= q0 + tq)
    def _():
        ot = acc_ref[0] / l_ref[0] - lam_ref[:, 0:1] * (acc_ref[1] / l_ref[1])
        ms = jnp.mean(ot * ot, axis=0, keepdims=True)
        o_ref[...] = ((ot * lax.rsqrt(ms + RMS_EPS)).T * g_ref[...]).astype(BF16)


def _attn_prompt(qat, qbt, kbf, vt, lam_vec, g_eff, batch, seq):
    tq = min(ATTN_TQ, seq)
    tk = min(ATTN_TK, seq)
    assert tq % tk == 0
    pairs = [(i, j) for i in range(seq // tq) for j in range(((i + 1) * tq) // tk)]
    qi_tab = jnp.asarray([a for a, _ in pairs], jnp.int32)
    ki_tab = jnp.asarray([b for _, b in pairs], jnp.int32)
    qspec = pl.BlockSpec((None, A_HD, tq), lambda b, h, p, qt, kt: (b, h, qt[p]))
    kspec = pl.BlockSpec((None, tk, A_HD), lambda b, h, p, qt, kt: (b, kt[p], h))
    vspec = pl.BlockSpec((None, A_HD, tk), lambda b, h, p, qt, kt: (b, h, kt[p]))
    ospec = pl.BlockSpec((None, tq, A_HD), lambda b, h, p, qt, kt: (b, qt[p], h))
    vec = pl.BlockSpec((1, A_HD), lambda b, h, p, qt, kt: (0, 0))
    out = pl.pallas_call(
        functools.partial(_attn_kernel, tq=tq, tk=tk),
        grid_spec=pltpu.PrefetchScalarGridSpec(
            num_scalar_prefetch=2,
            grid=(batch, A_HEADS, len(pairs)),
            in_specs=[qspec, qspec, kspec, vspec, vec, vec],
            out_specs=ospec,
            scratch_shapes=[
                pltpu.VMEM((2, 1, tq), F32), pltpu.VMEM((2, 1, tq), F32), pltpu.VMEM((2, A_HD, tq), F32),
            ],
        ),
        out_shape=jax.ShapeDtypeStruct((batch, seq, A_WIDTH), BF16),
        compiler_params=_cparams(("parallel", "parallel", "arbitrary")),
        name="attn_prompt",
    )(qi_tab, ki_tab, qat, qbt, kbf.reshape(batch, seq, A_WIDTH), vt, lam_vec, g_eff)
    return out.reshape(batch * seq, A_WIDTH)


def _attn_decode_kernel(pt_ref, *refs, n_pages, n_new, n_seq):
    del pt_ref
    qa_all, qb_all, kn_all, vn_all, lam_ref, g_ref = refs[:6]
    o_all = refs[6 + 2 * n_pages * n_seq]
    for i in range(n_seq):
        base = 6 + 2 * n_pages * i
        _attn_decode_one(qa_all.at[i], qb_all.at[i], kn_all.at[i], vn_all.at[i], lam_ref, g_ref,
                         refs[base:base + n_pages], refs[base + n_pages:base + 2 * n_pages], o_all.at[i], n_new)


def _attn_decode_one(qa_ref, qb_ref, kn_ref, vn_ref, lam_ref, g_ref, kp, vp, o_ref, n_new):
    n_pages = len(kp)
    rows = DEC_PAD
    grp = 2 * rows
    nq = A_HEADS * grp
    page_rows = kp[0].shape[0]
    head_lanes = [slice(hd * A_HD, (hd + 1) * A_HD) for hd in range(A_HEADS)]
    q_all = jnp.concatenate([r[:, lanes] for lanes in head_lanes for r in (qa_ref, qb_ref)], axis=0).astype(BF16)
    s = jnp.concatenate([_dot_nt(q_all, kp[j][...].astype(BF16)) for j in range(n_pages)], axis=1)
    row_head = _group_of(lax.broadcasted_iota(jnp.int32, s.shape, 0), grp)
    col_head = jnp.bitwise_and(lax.broadcasted_iota(jnp.int32, s.shape, 1), A_HEADS - 1)
    s = jnp.where(row_head == col_head, s, NEG_INF)
    s_new = _dot_nt(q_all, kn_ref[...].astype(BF16))
    r_new = lax.broadcasted_iota(jnp.int32, s_new.shape, 0)
    c_new = lax.broadcasted_iota(jnp.int32, s_new.shape, 1)
    tok = jnp.bitwise_and(r_new, rows - 1)
    visible = _group_of(c_new, A_HEADS) <= jnp.minimum(tok, n_new - 1)
    same_head = _group_of(r_new, grp) == jnp.bitwise_and(c_new, A_HEADS - 1)
    s_new = jnp.where(same_head, jnp.where(visible, s_new, NEG_INF), NEG_INF)
    m = jnp.maximum(jnp.max(s, axis=1, keepdims=True), jnp.max(s_new, axis=1, keepdims=True))
    pr = jnp.exp2(s - m)
    pr_new = jnp.exp2(s_new - m)
    inv_l = 1.0 / (jnp.sum(pr, axis=1, keepdims=True) + jnp.sum(pr_new, axis=1, keepdims=True))
    pr = pr * inv_l
    pr_new = pr_new * inv_l
    lam = lam_ref[:, 0:1]

    def diff(p):
        return jnp.concatenate([p[hd * grp:hd * grp + rows] - lam * p[hd * grp + rows:(hd + 1) * grp]
                                for hd in range(A_HEADS)], axis=0).astype(BF16)

    wgt = diff(pr)
    wgt_new = diff(pr_new)
    o = _dot(wgt_new, vn_ref[...].astype(BF16))
    for j in range(n_pages):
        o = o + _dot(wgt[:, j * page_rows:(j + 1) * page_rows], vp[j][...].astype(BF16))
    g = g_ref[...]
    for hd, lanes in enumerate(head_lanes):
        o_ref[:, lanes] = _rms(o[hd * rows:(hd + 1) * rows], g)


def _attn_decode(qa8, qb8, kn8, vn8, cache_k, cache_v, page_table, layer, lam_vec, g_eff, n_new):
    nb = qa8.shape[0]
    n_pages = page_table.shape[1]
    page_rows = cache_k.shape[2]
    n_seq = math.gcd(nb, DECODE_SEQS_PER_STEP)
    tok_spec = pl.BlockSpec((n_seq, DEC_PAD, A_WIDTH), lambda b, pt: (b, 0, 0))
    new_spec = pl.BlockSpec((n_seq, DEC_PAD * A_HEADS, A_HD), lambda b, pt: (b, 0, 0))
    vec = pl.BlockSpec((1, A_HD), lambda b, pt: (0, 0))

    def page_spec(i, j):
        return pl.BlockSpec((None, None, page_rows, A_HD), lambda b, pt: (layer, pt[b * n_seq + i, j], 0, 0))

    page_specs, page_args = [], []
    for i in range(n_seq):
        for cache in (cache_k, cache_v):
            page_specs += [page_spec(i, j) for j in range(n_pages)]
            page_args += [cache] * n_pages
    return pl.pallas_call(
        functools.partial(_attn_decode_kernel, n_pages=n_pages, n_new=n_new, n_seq=n_seq),
        grid_spec=pltpu.PrefetchScalarGridSpec(
            num_scalar_prefetch=1,
            grid=(nb // n_seq,),
            in_specs=[tok_spec, tok_spec, new_spec, new_spec, vec, vec] + page_specs,
            out_specs=tok_spec,
        ),
        out_shape=jax.ShapeDtypeStruct((nb, DEC_PAD, A_WIDTH), F32),
        compiler_params=_cparams(("arbitrary",)),
        name="attn_decode",
    )(page_table, qa8, qb8, kn8, vn8, lam_vec, g_eff, *page_args)


def _mm_kernel(*refs, n_terms):
    o_ref = refs[-1]
    acc = None
    for i in range(n_terms):
        t = _dot(refs[2 * i][...].astype(BF16), refs[2 * i + 1][...])
        acc = t if acc is None else acc + t
    o_ref[...] = acc


def _mm(terms, layer, tm=256, tn=1024):
    m = terms[0][0].shape[0]
    n = terms[0][1].shape[2]
    tm = min(tm, m)
    tn = min(tn, n)
    in_specs, args = [], []
    for x, w in terms:
        k = x.shape[1]
        in_specs += [pl.BlockSpec((tm, k), lambda i, j: (i, 0)),
                     pl.BlockSpec((None, k, tn), lambda i, j: (layer, 0, j))]
        args += [x, w]
    return pl.pallas_call(
        functools.partial(_mm_kernel, n_terms=len(terms)),
        grid=(m // tm, n // tn),
        in_specs=in_specs,
        out_specs=pl.BlockSpec((tm, tn), lambda i, j: (i, j)),
        out_shape=jax.ShapeDtypeStruct((m, n), F32),
        compiler_params=_cparams(("parallel", "parallel")),
        name="s5_mm",
    )(*args)


def _s5_scan_kernel(e_ref, a_ref, xin_ref, xlast_ref):
    nc = e_ref.shape[0]
    gp = S5_GP
    are = a_ref[:, :gp]
    aim = a_ref[:, gp:]

    def body(j, carry):
        xr, xi = carry
        xin_ref[pl.ds(j, 1), :gp] = xr
        xin_ref[pl.ds(j, 1), gp:] = xi
        er = e_ref[pl.ds(j, 1), :gp]
        ei = e_ref[pl.ds(j, 1), gp:]
        return are * xr - aim * xi + er, are * xi + aim * xr + ei

    zero = jnp.zeros((1, gp), F32)
    xr, xi = lax.fori_loop(0, nc, body, (zero, zero))
    xlast_ref[:, :gp] = xr
    xlast_ref[:, gp:] = xi


def _s5_scan(e, a_chunk, batch):
    nc = e.shape[0] // batch
    w = e.shape[1]
    e3 = e.reshape(batch, nc, w)
    xin, xlast = pl.pallas_call(
        _s5_scan_kernel,
        grid=(batch,),
        in_specs=[pl.BlockSpec((None, nc, w), lambda b: (b, 0, 0)), pl.BlockSpec((1, w), lambda b: (0, 0))],
        out_specs=[pl.BlockSpec((None, nc, w), lambda b: (b, 0, 0)), pl.BlockSpec((None, 1, w), lambda b: (b, 0, 0))],
        out_shape=[jax.ShapeDtypeStruct((batch, nc, w), F32), jax.ShapeDtypeStruct((batch, 1, w), F32)],
        compiler_params=_cparams(("parallel",)),
        name="s5_scan",
    )(e3, a_chunk)
    return xin.reshape(batch * nc, w), xlast.reshape(batch, w)


def _s5_step_kernel(e_ref, a_ref, x0_ref, o_ref):
    gp = S5_GP
    are, aim = a_ref[:, :gp], a_ref[:, gp:]
    xr, xi = x0_ref[:, :gp], x0_ref[:, gp:]
    o_ref[:, :gp] = are * xr - aim * xi + e_ref[:, :gp]
    o_ref[:, gp:] = are * xi + aim * xr + e_ref[:, gp:]


def _s5_step(e, a_chunk, x0):
    return pl.pallas_call(
        _s5_step_kernel,
        out_shape=jax.ShapeDtypeStruct(e.shape, F32),
        name="s5_step",
    )(e, a_chunk, x0)


def _s5_matrices(lam_re, lam_im, b_re, b_im, c_re, c_im, log_dt, chunk):
    hp = lax.Precision.HIGHEST
    g, p, cc, L = S5_GROUPS, S5_P, S5_GROUP, chunk
    dt = jnp.exp(log_dt)[:, None]
    zr, zi = lam_re * dt, lam_im * dt
    d = jnp.arange(L + 1, dtype=F32)[:, None, None]
    mag = jnp.exp(zr[None] * d)
    pw_r, pw_i = mag * jnp.cos(zi[None] * d), mag * jnp.sin(zi[None] * d)
    ar, ai = pw_r[1] - 1.0, pw_i[1]
    den = lam_re * lam_re + lam_im * lam_im
    qr, qi = (ar * lam_re + ai * lam_im) / den, (ai * lam_re - ar * lam_im) / den
    bb_r = qr[..., None] * b_re - qi[..., None] * b_im
    bb_i = qr[..., None] * b_im + qi[..., None] * b_re
    def spread(a):
        x = a.shape[-1]
        lane_group = jnp.arange(g * x, dtype=jnp.int32) // x
        own = lane_group[None, None, None, :] == jnp.arange(g, dtype=jnp.int32)[None, :, None, None]
        return jnp.where(own, jnp.tile(a, (1, 1, 1, g)), 0.0)

    d_rev = jnp.arange(L - 1, -1, -1).astype(F32)[:, None, None]
    mag_rev = jnp.exp(zr[None] * d_rev)
    rev_r, rev_i = mag_rev * jnp.cos(zi[None] * d_rev), mag_rev * jnp.sin(zi[None] * d_rev)
    e_r = rev_r[..., None] * bb_r[None] - rev_i[..., None] * bb_i[None]
    e_i = rev_r[..., None] * bb_i[None] + rev_i[..., None] * bb_r[None]

    def bend_of(e):
        return spread(jnp.transpose(e, (0, 1, 3, 2))).reshape(L * g * cc, g * p)

    bend = jnp.concatenate([bend_of(e_r), bend_of(e_i)], axis=1)
    cp_r = c_re[None] * pw_r[:L, :, None, :] - c_im[None] * pw_i[:L, :, None, :]
    cp_i = c_re[None] * pw_i[:L, :, None, :] + c_im[None] * pw_r[:L, :, None, :]
    kd = (jnp.einsum('dgcp,gpe->dgce', cp_r, bb_r, precision=hp)
          - jnp.einsum('dgcp,gpe->dgce', cp_i, bb_i, precision=hp))
    kbd = spread(jnp.transpose(kd, (0, 1, 3, 2)))
    zero_blk = jnp.zeros((1,) + kbd.shape[1:], F32)
    cols = [jnp.concatenate([kbd[t::-1]] + [zero_blk] * (L - 1 - t), axis=0).reshape(L * g * cc, g * cc)
            for t in range(L)]
    mtoep = jnp.concatenate(cols, axis=1)
    c1_r = c_re[None] * pw_r[1:, :, None, :] - c_im[None] * pw_i[1:, :, None, :]
    c1_i = c_re[None] * pw_i[1:, :, None, :] + c_im[None] * pw_r[1:, :, None, :]

    def ccar_of(c1):
        per_t = spread(jnp.transpose(c1, (0, 1, 3, 2)))
        return jnp.concatenate([per_t[t].reshape(g * p, g * cc) for t in range(L)], axis=1)

    ccar = jnp.concatenate([ccar_of(c1_r), ccar_of(-c1_i)], axis=0)
    a_chunk = jnp.concatenate([pw_r[L].reshape(1, g * p), pw_i[L].reshape(1, g * p)], axis=1)
    return bend.astype(BF16), mtoep.astype(BF16), ccar.astype(BF16), a_chunk


def _gla_kernel(*refs, lb, n_blocks, n_chains, has_s0):
    nc = n_chains
    qk_refs, v_refs, bc_refs, gg_refs = (refs[i * nc:(i + 1) * nc] for i in range(4))
    gn_ref = refs[4 * nc]
    pos = 4 * nc + 1
    s0_ref = refs[pos] if has_s0 else None
    pos += int(has_s0)
    o_refs = refs[pos:pos + nc]
    s_out_ref, st_ref, obuf_ref = refs[pos + nc:]
    t = pl.program_id(1)
    kw, vw = GLA_QK, C_WIDTH
    def same_head(shape, row_size, col_size):
        r = _group_of(lax.broadcasted_iota(jnp.int32, shape, 0), row_size)
        c = _group_of(lax.broadcasted_iota(jnp.int32, shape, 1), col_size)
        return r == c

    bd_mask = same_head((vw, kw), GLA_DV, GLA_DK)

    @pl.when(t == 0)
    def _():
        if has_s0:
            for c in range(nc):
                s0t = s0_ref[c].T
                st_ref[c] = jnp.where(bd_mask, jnp.concatenate([s0t] * GLA_HEADS, axis=0), 0.0)
        else:
            st_ref[...] = jnp.zeros(st_ref.shape, F32)

    head_expand = jnp.where(same_head((kw, vw), GLA_DK, GLA_DV), 1.0, 0.0).astype(BF16)
    row = lax.broadcasted_iota(jnp.int32, (lb, kw), 0)
    scale = GLA_DK ** -0.5

    def block(c, r0):
        q = qk_refs[c][pl.ds(r0, lb), 0:kw] * scale
        k = qk_refs[c][pl.ds(r0, lb), kw:2 * kw]
        v = v_refs[c][pl.ds(r0, lb), :]
        bc = bc_refs[c][pl.ds(r0, lb), :]
        bl = bc[lb - 1:lb, :]
        st = st_ref[c]
        o = _dot_nt((q * jnp.exp(bc)).astype(BF16), st.astype(BF16))
        parts = []
        for s in range(lb):
            dec = jnp.exp(jnp.minimum(bc - bc[s:s + 1, :], 0.0))
            parts.append(jnp.where(row >= s, q * k[s:s + 1, :] * dec, 0.0))
        att = _dot(jnp.concatenate(parts, axis=0).astype(BF16), head_expand)
        for s in range(lb):
            o = o + att[s * lb:(s + 1) * lb, :] * v[s:s + 1, :]
        obuf_ref[c, pl.ds(r0, lb), :] = o
        ke = (k * jnp.exp(bl - bc)).astype(BF16)
        upd = _dot_tn(v.astype(BF16), ke)
        st_ref[c] = jnp.exp(bl) * st + jnp.where(bd_mask, upd, 0.0)

    def body(i, carry):
        r0 = pl.multiple_of(i * lb, lb)
        for c in range(nc):
            block(c, r0)
        return carry

    lax.fori_loop(0, n_blocks, body, 0)

    seg_ones = jnp.where(same_head((vw, vw), GLA_DV, GLA_DV), 1.0, 0.0).astype(BF16)
    for c in range(nc):
        o = obuf_ref[c]
        ms = _dot((o * o).astype(BF16), seg_ones) * (1.0 / GLA_DV)
        gg = gg_refs[c][...]
        o_refs[c][...] = (o * lax.rsqrt(ms + RMS_EPS) * gn_ref[...] * (gg * _sigmoid(gg))).astype(BF16)

    @pl.when(t == pl.num_programs(1) - 1)
    def _():
        for c in range(nc):
            st = st_ref[c]
            comp = st[0:GLA_DV]
            for hd in range(1, GLA_HEADS):
                comp = comp + st[hd * GLA_DV:(hd + 1) * GLA_DV]
            s_out_ref[c] = comp.T


def _gla(qkg, vg, bc, gg, gn_vec, s0, batch, seq, lb, tile, n_chains):
    tile = min(tile, seq)
    nt = seq // tile
    nc = n_chains
    assert batch % nc == 0

    def row(w, c):
        return pl.BlockSpec((tile, w), lambda g, t: ((g * nc + c) * nt + t, 0))

    st_spec = pl.BlockSpec((nc, GLA_QK, GLA_DV), lambda g, t: (g, 0, 0))
    in_specs, args = [], []
    for arr, w in ((qkg, 2 * GLA_QK), (vg, C_WIDTH), (bc, GLA_QK), (gg, C_WIDTH)):
        in_specs += [row(w, c) for c in range(nc)]
        args += [arr] * nc
    in_specs.append(pl.BlockSpec((1, C_WIDTH), lambda g, t: (0, 0)))
    args.append(gn_vec)
    if s0 is not None:
        in_specs.append(st_spec)
        args.append(s0)
    res = pl.pallas_call(
        functools.partial(_gla_kernel, lb=lb, n_blocks=tile // lb, n_chains=nc, has_s0=s0 is not None),
        grid=(batch // nc, nt),
        in_specs=in_specs,
        out_specs=[pl.BlockSpec((tile, C_WIDTH), lambda g, t: (g * nt + t, 0))] * nc + [st_spec],
        out_shape=[jax.ShapeDtypeStruct((batch // nc * seq, C_WIDTH), BF16)] * nc
        + [jax.ShapeDtypeStruct((batch, GLA_QK, GLA_DV), F32)],
        scratch_shapes=[pltpu.VMEM((nc, C_WIDTH, GLA_QK), F32), pltpu.VMEM((nc, tile, C_WIDTH), F32)],
        compiler_params=_cparams(("parallel", "arbitrary")),
        name="gla",
    )(*args)
    return res[:nc], res[nc]


def _outproj_kernel(x_ref, oa_ref, yc_ref, u_ref, d_ref, gw_ref, gb_ref, oc_ref, w_ref, gt_ref, gpost_ref, o_ref,
                    ya_ref, yb_ref, *, s5_chunk):
    tm = x_ref.shape[0]
    for s in range(s5_chunk):
        for half, part_ref in enumerate((ya_ref, yb_ref)):
            c0 = s * B_WIDTH + half * V7X_LANES
            part_ref[pl.ds(s, tm // s5_chunk, stride=s5_chunk), :] = yc_ref[:, c0:c0 + V7X_LANES]
    y = jnp.concatenate([ya_ref[...], yb_ref[...]], axis=1) + d_ref[...] * u_ref[...]
    g = 0.5 * y * (1.0 + jnp.tanh(math.sqrt(2.0 / math.pi) * (y + 0.044715 * (y * y * y))))
    ob = g * _sigmoid(_dot(g.astype(BF16), gw_ref[...]) + gb_ref[...])
    a0, a1 = A_WIDTH, A_WIDTH + B_WIDTH
    mixed = (_dot(oa_ref[...], w_ref[0:a0, :]) + _dot(ob.astype(BF16), w_ref[a0:a1, :])
             + _dot(oc_ref[...], w_ref[a1:, :]))
    o_ref[...] = x_ref[...] + (1.0 + gt_ref[...]) * _rms(mixed, gpost_ref[...])


def _outproj(x, oa, yc, u, s5_d, glu_w, glu_b, oc, w_out, mod, layer, norm_post, tm, s5_chunk):
    n, d = x.shape
    row = lambda w: pl.BlockSpec((tm, w), lambda i: (i, 0))
    lvec = lambda w: pl.BlockSpec((None, 1, w), lambda i: (layer, 0, 0))
    return pl.pallas_call(
        functools.partial(_outproj_kernel, s5_chunk=s5_chunk),
        scratch_shapes=[pltpu.VMEM((tm, V7X_LANES), F32), pltpu.VMEM((tm, V7X_LANES), F32)],
        grid=(n // tm,),
        in_specs=[
            row(d), row(A_WIDTH), pl.BlockSpec((tm // s5_chunk, s5_chunk * B_WIDTH), lambda i: (i, 0)),
            row(B_WIDTH), lvec(B_WIDTH),
            _resident((None, B_WIDTH, B_WIDTH), lambda i: (layer, 0, 0)), lvec(B_WIDTH),
            row(C_WIDTH), _resident((None, d, d), lambda i: (layer, 0, 0)),
            mod.spec(1, 2, tm), pl.BlockSpec((None, None, 1, d), lambda i: (layer, 1, 0, 0)),
        ],
        out_specs=row(d),
        out_shape=jax.ShapeDtypeStruct((n, d), F32),
        compiler_params=_cparams(("parallel",)),
        name="outproj",
    )(x, oa, yc, u, s5_d, glu_w, glu_b, oc, w_out, mod.arr, norm_post)


def kernel(x_prompt, x_sample, cache_k, cache_v, state_s5_re, state_s5_im, state_gla, page_table, c_prompt, c_sample, ada_w, ada_b, norm_pre, norm_post, ffn1_wi, ffn1_wo, ffn2_wi, ffn2_wo, w_in, w_out, lam_q1, lam_k1, lam_q2, lam_k2, subln_g, s5_lam_re, s5_lam_im, s5_b_re, s5_b_im, s5_c_re, s5_c_im, s5_d, s5_log_dt, s5_glu_w, s5_glu_b, gla_gate_w2, gla_gate_b, gla_norm_g):
    depth = ada_w.shape[0]
    batch, seq, d = x_prompt.shape
    dec_b, dec_t, _ = x_sample.shape
    n_pages = page_table.shape[1]
    page = cache_k.shape[2]
    past_len = n_pages * page
    n_p, n_s = batch * seq, dec_b * dec_t
    tm_p, tm_s = min(ROW_TILE, seq), min(ROW_TILE, n_s)
    tf_p = min(FFN_TILE, seq)

    n_c = batch + dec_b
    pad_c = (-n_c) % 8
    c_all = jnp.concatenate([c_prompt, c_sample, jnp.zeros((pad_c, d), F32)], axis=0)
    mod_all = _ada_mod(c_all, ada_w, ada_b)
    mod_p = mod_all[:, :batch].reshape(depth, batch, 1, N_SUB * 3 * d)
    mod_s = jnp.repeat(mod_all[:, batch:n_c], dec_t, axis=1)

    bf = lambda a: a.astype(BF16)
    wi1, wo1, wi2, wo2, w_out_b = bf(ffn1_wi), bf(ffn1_wo), bf(ffn2_wi), bf(ffn2_wo), bf(w_out)
    in_dim = w_in.shape[2]
    w_in_b = bf(jnp.pad(w_in, ((0, 0), (0, 0), (0, V7X_LANES - GLA_RANK))))
    assert w_in_b.shape[2] == in_dim - GLA_RANK + V7X_LANES
    w2p = bf(jnp.pad(gla_gate_w2, ((0, 0), (0, V7X_LANES - GLA_RANK), (0, 0))))
    b2 = gla_gate_b.reshape(depth, 1, GLA_QK)
    glu_w = bf(s5_glu_w)
    glu_b = s5_glu_b.reshape(depth, 1, B_WIDTH)
    s5_d3 = s5_d.reshape(depth, 1, B_WIDTH)
    npre = norm_pre.reshape(depth, N_SUB, 1, d)
    npost = norm_post.reshape(depth, N_SUB, 1, d)
    gn_vec = jnp.tile(gla_norm_g, (1, GLA_HEADS)).reshape(depth, 1, C_WIDTH)

    tabs_p = _rope_tables(jnp.arange(seq, dtype=jnp.int32))
    tabs_s = _rope_tables(past_len + jnp.arange(n_s, dtype=jnp.int32) % dec_t)

    cache_k4 = cache_k.reshape(depth, cache_k.shape[1], page * A_HEADS, A_HD)
    cache_v4 = cache_v.reshape(depth, cache_v.shape[1], page * A_HEADS, A_HD)

    def pad_tokens(a, edge=False):
        a3 = a.reshape(dec_b, dec_t, a.shape[-1])
        return jnp.pad(a3, ((0, 0), (0, DEC_PAD - dec_t), (0, 0)), mode='edge' if edge else 'constant')

    s5_params = (s5_lam_re, s5_lam_im, s5_b_re, s5_b_im, s5_c_re, s5_c_im, s5_log_dt)
    s5_mats_p = jax.vmap(functools.partial(_s5_matrices, chunk=min(S5_CHUNK_PROMPT, seq)))(*s5_params)
    s5_mats_s = jax.vmap(functools.partial(_s5_matrices, chunk=dec_t))(*s5_params)

    yp = x_prompt.reshape(n_p, d)
    ys = x_sample.reshape(n_s, d)
    outs_p, outs_s = [], []
    for l in range(depth):
        mp = _Mod(mod_p, l, seq, d)
        ms = _Mod(mod_s, l, n_s, d)
        lam_init = 0.8 - 0.6 * math.exp(-0.3 * l)
        lam = (jnp.exp(jnp.sum(lam_q1[l] * lam_k1[l])) - jnp.exp(jnp.sum(lam_q2[l] * lam_k2[l])) + lam_init)
        lam_vec = jnp.full((1, A_HD), lam, F32)
        g_eff = (subln_g[l] * (1.0 - lam_init)).reshape(1, A_HD)

        yp = _ffn(yp, mp, 0, l, npre, npost, wi1, wo1, tf_p)
        ys = _ffn(ys, ms, 0, l, npre, npost, wi1, wo1, tm_s)

        lp = min(S5_CHUNK_PROMPT, seq)
        (qat_p, qbt_p, k32_p, kbf_p, v32_p, vt_p, u_p, uc_p, qkg_p, vg_p, gg_p, bc_p) = _inproj(
            yp, mp, l, npre, w_in_b, w2p, b2, tabs_p, batch, tm_p, GLA_BLOCK_PROMPT, lp, True)
        (qa_s, qb_s, k32_s, v32_s, u_s, uc_s, qkg_s, vg_s, gg_s, bc_s) = _inproj(
            ys, ms, l, npre, w_in_b, w2p, b2, tabs_s, 1, tm_s, dec_t, dec_t, False)

        oa_p = _attn_prompt(qat_p, qbt_p, kbf_p, vt_p, lam_vec, g_eff, batch, seq)

        def pad_new_rows(a):
            a3 = a.reshape(dec_b, dec_t * A_HEADS, A_HD)
            return jnp.pad(a3, ((0, 0), (0, (DEC_PAD - dec_t) * A_HEADS), (0, 0)))

        oa_s8 = _attn_decode(pad_tokens(qa_s), pad_tokens(qb_s), pad_new_rows(k32_s), pad_new_rows(v32_s),
                             cache_k4, cache_v4, page_table, l, lam_vec, g_eff, dec_t)
        oa_s = oa_s8[:, :dec_t].reshape(n_s, A_WIDTH).astype(BF16)

        bend, mtoep, ccar, a_chunk = s5_mats_p
        e_p = _mm([(uc_p, bend)], l)
        xin_p, xlast_p = _s5_scan(e_p, a_chunk[l], batch)
        yc_p = _mm([(uc_p, mtoep), (xin_p, ccar)], l)
        bend4, mtoep4, ccar4, a_chunk4 = s5_mats_s
        x0 = jnp.concatenate([state_s5_re[l].reshape(dec_b, S5_GP), state_s5_im[l].reshape(dec_b, S5_GP)], axis=1)
        xlast_s = _s5_step(_mm([(uc_s, bend4)], l), a_chunk4[l], x0)
        yc_s = _mm([(uc_s, mtoep4), (x0, ccar4)], l)

        ncp = math.gcd(batch, GLA_CHAINS_PROMPT)
        oc_parts, gla_p = _gla(qkg_p, vg_p, bc_p, gg_p, gn_vec[l], None, batch, seq, GLA_BLOCK_PROMPT, GLA_TILE, ncp)
        oc_p = jnp.stack([o.reshape(batch // ncp, seq, C_WIDTH) for o in oc_parts], axis=1).reshape(n_p, C_WIDTH)
        flat8 = lambda a, edge=False: pad_tokens(a, edge).reshape(dec_b * DEC_PAD, a.shape[-1])
        ncs = math.gcd(dec_b, GLA_CHAINS_DECODE)
        oc_parts, gla_s = _gla(flat8(qkg_s), flat8(vg_s), flat8(bc_s, True), flat8(gg_s), gn_vec[l],
                               state_gla[l].reshape(dec_b, GLA_QK, GLA_DV), dec_b, DEC_PAD, DEC_PAD, DEC_PAD, ncs)
        oc_s8 = jnp.stack([o.reshape(dec_b // ncs, DEC_PAD, C_WIDTH) for o in oc_parts], axis=1)
        oc_s = oc_s8.reshape(dec_b, DEC_PAD, C_WIDTH)[:, :dec_t].reshape(n_s, C_WIDTH)

        yp = _outproj(yp, oa_p, yc_p, u_p, s5_d3, glu_w, glu_b, oc_p, w_out_b, mp, l, npost, tm_p, lp)
        ys = _outproj(ys, oa_s, yc_s, u_s, s5_d3, glu_w, glu_b, oc_s, w_out_b, ms, l, npost, tm_s, dec_t)

        yp = _ffn(yp, mp, 2, l, npre, npost, wi2, wo2, tf_p)
        ys = _ffn(ys, ms, 2, l, npre, npost, wi2, wo2, tm_s)

        outs_p.append((k32_p.reshape(batch, seq, A_HEADS, A_HD), v32_p.reshape(batch, seq, A_HEADS, A_HD),
                       xlast_p[:, :S5_GP].reshape(batch, S5_GROUPS, S5_P),
                       xlast_p[:, S5_GP:].reshape(batch, S5_GROUPS, S5_P),
                       gla_p.reshape(batch, GLA_HEADS, GLA_DK, GLA_DV)))
        outs_s.append((k32_s.reshape(dec_b, dec_t, A_HEADS, A_HD), v32_s.reshape(dec_b, dec_t, A_HEADS, A_HD),
                       xlast_s[:, :S5_GP].reshape(dec_b, S5_GROUPS, S5_P),
                       xlast_s[:, S5_GP:].reshape(dec_b, S5_GROUPS, S5_P),
                       gla_s.reshape(dec_b, GLA_HEADS, GLA_DK, GLA_DV)))

    kp, vp, srp, sip, gp = [jnp.stack(z) for z in zip(*outs_p)]
    ks, vs, srs, sis, gs = [jnp.stack(z) for z in zip(*outs_s)]
    return (yp.reshape(batch, seq, d), ys.reshape(dec_b, dec_t, d), kp, vp, srp, sip, gp, ks, vs, srs, sis, gs)
```
